```python
import jax, jax.numpy as jnp
from jax import lax
import numpy as np

D_MODEL = 1024
BATCH = 16
SEQ = 256
DEPTH = 2
DEC_BATCH = 4
DEC_SEQ = 1024
PAST_LEN = 256

GRID_W = 64
BLOCK = 128
ROPE_BASE = 10000.0
EPS = 1e-6
NEG_INF = -1e30
N_MOD = 6

POOL_WINDOWS = (2, 4, 8, 16)
POOL_GROUPS = 4
POOL_GROUP_DIM = 128
POOL_WIDTH = POOL_GROUPS * POOL_GROUP_DIM
SWA_HEADS = 8
SWA_KV_HEADS = 2
SWA_GROUP = SWA_HEADS // SWA_KV_HEADS
SWA_HEAD_DIM = 64
SWA_WINDOW = 128
MLA_HEADS = 8
MLA_Q_RANK = 384
MLA_KV_RANK = 256
MLA_NOPE_DIM = 64
MLA_ROPE_DIM = 32
MLA_QK_DIM = MLA_NOPE_DIM + MLA_ROPE_DIM
MLA_V_DIM = 64
N_BRANCHES = 3
BRANCH_A_WIDTH = POOL_WIDTH
BRANCH_B_WIDTH = SWA_HEADS * SWA_HEAD_DIM
BRANCH_C_WIDTH = MLA_HEADS * MLA_V_DIM
IN_WIDTHS = (POOL_WIDTH, SWA_HEADS * SWA_HEAD_DIM, SWA_KV_HEADS * SWA_HEAD_DIM,
             SWA_KV_HEADS * SWA_HEAD_DIM, MLA_Q_RANK, MLA_KV_RANK, MLA_ROPE_DIM,
             N_BRANCHES * D_MODEL)
IN_WIDTH = sum(IN_WIDTHS)
N_EXPERTS = 16
EXPERT_HIDDEN = 1024
CAPACITY_FACTOR = 2

kernel_name = 'hybrid_dit_pool_swa_mla_ec'


def rmsnorm(x, g):
    xf = x.astype(jnp.float32)
    y = xf * lax.rsqrt(jnp.mean(xf * xf, axis=-1, keepdims=True) + EPS)
    return (y * g.astype(jnp.float32)).astype(x.dtype)


def split_in(z):
    outs, off = [], 0
    for w in IN_WIDTHS:
        outs.append(z[..., off:off + w])
        off += w
    return outs


def grid_positions(n):
    rows = n // GRID_W
    row = jnp.repeat(jnp.arange(rows), GRID_W)
    col = jnp.tile(jnp.arange(GRID_W), rows)
    return row, col


def rope_1d(x, pos):
    half = x.shape[-1] // 2
    freqs = ROPE_BASE ** (-jnp.arange(half, dtype=jnp.float32) / half)
    ang = pos.astype(jnp.float32)[:, None] * freqs[None, :]
    cos = jnp.cos(ang)[None, :, None, :]
    sin = jnp.sin(ang)[None, :, None, :]
    xf = x.astype(jnp.float32)
    x1, x2 = xf[..., :half], xf[..., half:]
    return jnp.concatenate([x1 * cos - x2 * sin, x1 * sin + x2 * cos], axis=-1).astype(x.dtype)


def rope_2d(x):
    row, col = grid_positions(x.shape[1])
    h = x.shape[-1] // 2
    return jnp.concatenate([rope_1d(x[..., :h], row), rope_1d(x[..., h:], col)], axis=-1)


def rope_mla(x):
    return jnp.concatenate([x[..., :MLA_NOPE_DIM], rope_2d(x[..., MLA_NOPE_DIM:])], axis=-1)


def pool_mixer(u, w_pool, pool_scale):
    n = u.shape[1]
    uf = u.astype(jnp.float32)
    cs = jnp.concatenate([jnp.zeros_like(uf[:, :1]), jnp.cumsum(uf, axis=1)], axis=1)
    t = jnp.arange(n)
    outs = []
    for g, w in enumerate(POOL_WINDOWS):
        sl = slice(g * POOL_GROUP_DIM, (g + 1) * POOL_GROUP_DIM)
        lo = jnp.clip(t - w // 2, 0, n)
        hi = jnp.clip(t + w // 2, 0, n)
        cnt = (hi - lo).astype(jnp.float32)[None, :, None]
        mean = (cs[:, hi, sl] - cs[:, lo, sl]) / cnt
        outs.append(jnp.einsum('bnc,cd->bnd', (mean - uf[..., sl]).astype(u.dtype), w_pool[g]))
    return jnp.concatenate(outs, axis=-1) * pool_scale


def dense_attn(q, k, v, sink=None):
    b, n, kvh, g, d = q.shape
    nb = n // BLOCK
    qb = jnp.moveaxis(q.reshape(b, nb, BLOCK, kvh, g, d), 1, 0)
    scale = d ** -0.5

    def one(qblk):
        s = jnp.einsum('bqhgd,bkhd->bhgqk', qblk, k).astype(jnp.float32) * scale
        if sink is None:
            p = jax.nn.softmax(s, axis=-1)
        else:
            sk = jnp.broadcast_to(sink.astype(jnp.float32)[None, :, :, None, None], s.shape[:-1] + (1,))
            p = jax.nn.softmax(jnp.concatenate([s, sk], axis=-1), axis=-1)[..., :-1]
        return jnp.einsum('bhgqk,bkhd->bqhgd', p.astype(v.dtype), v)

    o = lax.map(one, qb)
    return jnp.moveaxis(o, 0, 1).reshape(b, n, kvh * g * v.shape[-1])


def banded_attn(q, k, v, k_ctx, v_ctx, sink):
    b, n, kvh, g, d = q.shape
    nb = n // BLOCK
    qb = q.reshape(b, nb, BLOCK, kvh, g, d)

    def band(x):
        xp = jnp.pad(x, ((0, 0), (BLOCK, BLOCK), (0, 0), (0, 0))).reshape(b, nb + 2, BLOCK, kvh, x.shape[-1])
        return jnp.concatenate([xp[:, :-2], xp[:, 1:-1], xp[:, 2:]], axis=2)

    kw, vw = band(k), band(v)
    qpos = jnp.arange(nb)[:, None] * BLOCK + jnp.arange(BLOCK)[None, :]
    kpos = (jnp.arange(nb)[:, None] - 1) * BLOCK + jnp.arange(3 * BLOCK)[None, :]
    kp = kpos[:, None, :]
    mask = (jnp.abs(kp - qpos[:, :, None]) <= SWA_WINDOW) & (kp >= 0) & (kp < n)
    scale = d ** -0.5
    s_loc = jnp.einsum('bnqhgd,bnkhd->bhgnqk', qb, kw).astype(jnp.float32) * scale
    s_loc = jnp.where(mask, s_loc, NEG_INF)
    s_ctx = jnp.einsum('bnqhgd,bkhd->bhgnqk', qb, k_ctx).astype(jnp.float32) * scale
    sk = jnp.broadcast_to(sink.astype(jnp.float32)[None, :, :, None, None, None], s_loc.shape[:-1] + (1,))
    p = jax.nn.softmax(jnp.concatenate([s_loc, s_ctx, sk], axis=-1), axis=-1)
    wl = 3 * BLOCK
    p_loc = p[..., :wl].astype(v.dtype)
    p_ctx = p[..., wl:-1].astype(v.dtype)
    o = (jnp.einsum('bhgnqk,bnkhd->bnqhgd', p_loc, vw)
         + jnp.einsum('bhgnqk,bkhd->bnqhgd', p_ctx, v_ctx))
    return o.reshape(b, n, kvh * g * v.shape[-1])


def mla_kv(ckv, krope, w_kv_b, k_g):
    b, m, _ = ckv.shape
    kv = jnp.einsum('bmr,rf->bmf', ckv, w_kv_b).reshape(b, m, MLA_HEADS, MLA_NOPE_DIM + MLA_V_DIM)
    kr = jnp.broadcast_to(krope[:, :, None, :], (b, m, MLA_HEADS, MLA_ROPE_DIM))
    k = rmsnorm(jnp.concatenate([kv[..., :MLA_NOPE_DIM], kr], axis=-1), k_g)
    return k, kv[..., MLA_NOPE_DIM:]


def token_mixing(h, lp, ctx=None):
    b, n, _ = h.shape
    z = jnp.einsum('bnd,df->bnf', h, lp['w_in'])
    u_a, q_b, k_b, v_b, q_c, kv_c, kr_c, gts = split_in(z)
    ya = pool_mixer(u_a, lp['pool_w'], lp['pool_scale'])
    qs = rmsnorm(q_b.reshape(b, n, SWA_HEADS, SWA_HEAD_DIM), lp['swa_q_g'])
    ks = rmsnorm(k_b.reshape(b, n, SWA_KV_HEADS, SWA_HEAD_DIM), lp['swa_k_g'])
    vs = v_b.reshape(b, n, SWA_KV_HEADS, SWA_HEAD_DIM)
    sink = lp['sink'].reshape(SWA_KV_HEADS, SWA_GROUP)
    ckv = rmsnorm(kv_c, lp['mla_kv_norm_g'])
    qm = jnp.einsum('bnr,rf->bnf', rmsnorm(q_c, lp['mla_q_norm_g']), lp['w_q_b'])
    qm = rmsnorm(qm.reshape(b, n, MLA_HEADS, MLA_QK_DIM), lp['mla_q_g'])
    km, vm = mla_kv(ckv, kr_c, lp['w_kv_b'], lp['mla_k_g'])
    if ctx is None:
        yb = dense_attn(qs.reshape(b, n, SWA_KV_HEADS, SWA_GROUP, SWA_HEAD_DIM), ks, vs, sink)
        yc = dense_attn(qm[:, :, :, None, :], km, vm)
        new = (ks, vs, ckv, kr_c)
    else:
        ck, cv, cckv, ckr = ctx
        qs, ks = rope_2d(qs), rope_2d(ks)
        yb = banded_attn(qs.reshape(b, n, SWA_KV_HEADS, SWA_GROUP, SWA_HEAD_DIM), ks, vs, ck, cv, sink)
        qm, km = rope_mla(qm), rope_mla(km)
        kmc, vmc = mla_kv(cckv, ckr, lp['w_kv_b'], lp['mla_k_g'])
        yc = dense_attn(qm[:, :, :, None, :], jnp.concatenate([km, kmc], axis=1),
                        jnp.concatenate([vm, vmc], axis=1))
        new = None
    gate = jax.nn.sigmoid(gts.astype(jnp.float32)).astype(h.dtype).reshape(b, n, N_BRANCHES, D_MODEL)
    merged = (gate[:, :, 0] * jnp.einsum('bnc,cd->bnd', ya, lp['w_br_a'])
              + gate[:, :, 1] * jnp.einsum('bnc,cd->bnd', yb, lp['w_br_b'])
              + gate[:, :, 2] * jnp.einsum('bnc,cd->bnd', yc, lp['w_br_c']))
    return jnp.einsum('bnd,de->bne', merged, lp['w_out']), new


def ec_moe(h, w_router, w_gate, w_up, w_down):
    b, n, d = h.shape
    cap = CAPACITY_FACTOR * n // N_EXPERTS
    aff = jax.nn.softmax(jnp.einsum('bnd,de->ben', h, w_router).astype(jnp.float32), axis=1)
    vals, idx = lax.top_k(aff, cap)
    xg = jax.vmap(lambda hb, ib: hb[ib])(h, idx)
    a = jnp.einsum('becd,edf->becf', xg, w_gate)
    u = jnp.einsum('becd,edf->becf', xg, w_up)
    o = jnp.einsum('becf,efd->becd', jax.nn.silu(a) * u, w_down) * vals[..., None].astype(h.dtype)
    return jax.vmap(lambda ib, ob: jnp.zeros((n, d), ob.dtype).at[ib.reshape(-1)].add(ob.reshape(-1, d)))(idx, o)


def block(x, mod, lp, ctx=None):
    sh1, sc1, g1, sh2, sc2, g2 = jnp.split(mod, N_MOD, axis=-1)
    h = rmsnorm(x, lp['norm1_g']) * (1 + sc1) + sh1
    y, new = token_mixing(h, lp, ctx)
    x = x + g1 * y
    h = rmsnorm(x, lp['norm2_g']) * (1 + sc2) + sh2
    x = x + g2 * ec_moe(h, lp['w_router'], lp['w_gate'], lp['w_up'], lp['w_down'])
    return x, new


def setup_inputs(seed: int = 0) -> dict:
    key = jax.random.key(seed)
    ks = jax.random.split(key, 40)
    f32 = jnp.float32

    def nrm(i, shape, scale=1.0):
        return jax.random.normal(ks[i], shape, f32) * scale

    L = DEPTH
    return {
        'x_prompt': nrm(0, (BATCH, SEQ, D_MODEL)),
        'x_sample': nrm(1, (DEC_BATCH, DEC_SEQ, D_MODEL)),
        'cache_swa_k': nrm(2, (DEC_BATCH, L, PAST_LEN, SWA_KV_HEADS, SWA_HEAD_DIM)),
        'cache_swa_v': nrm(3, (DEC_BATCH, L, PAST_LEN, SWA_KV_HEADS, SWA_HEAD_DIM)),
        'cache_mla_ckv': nrm(4, (DEC_BATCH, L, PAST_LEN, MLA_KV_RANK)),
        'cache_mla_krope': nrm(5, (DEC_BATCH, L, PAST_LEN, MLA_ROPE_DIM)),
        'c': nrm(6, (DEC_BATCH, D_MODEL)),
        'c_ctx': nrm(7, (D_MODEL,)),
        'w_mod': nrm(8, (L, D_MODEL, N_MOD * D_MODEL), 0.5 * D_MODEL ** -0.5),
        'b_mod': nrm(9, (L, N_MOD * D_MODEL), 0.1),
        'norm1_g': 1.0 + nrm(10, (L, D_MODEL), 0.1),
        'norm2_g': 1.0 + nrm(11, (L, D_MODEL), 0.1),
        'w_in': nrm(12, (L, D_MODEL, IN_WIDTH), D_MODEL ** -0.5),
        'pool_w': nrm(13, (L, POOL_GROUPS, POOL_GROUP_DIM, POOL_GROUP_DIM), POOL_GROUP_DIM ** -0.5),
        'pool_scale': 1.0 + nrm(14, (L, POOL_WIDTH), 0.1),
        'swa_q_g': 1.0 + nrm(15, (L, SWA_HEAD_DIM), 0.1),
        'swa_k_g': 1.0 + nrm(16, (L, SWA_HEAD_DIM), 0.1),
        'sink': nrm(17, (L, SWA_HEADS), 0.5),
        'mla_q_norm_g': 1.0 + nrm(18, (L, MLA_Q_RANK), 0.1),
        'w_q_b': nrm(19, (L, MLA_Q_RANK, MLA_HEADS * MLA_QK_DIM), MLA_Q_RANK ** -0.5),
        'mla_kv_norm_g': 1.0 + nrm(20, (L, MLA_KV_RANK), 0.1),
        'w_kv_b': nrm(21, (L, MLA_KV_RANK, MLA_HEADS * (MLA_NOPE_DIM + MLA_V_DIM)), MLA_KV_RANK ** -0.5),
        'mla_q_g': 1.0 + nrm(22, (L, MLA_QK_DIM), 0.1),
        'mla_k_g': 1.0 + nrm(23, (L, MLA_QK_DIM), 0.1),
        'w_br_a': nrm(24, (L, BRANCH_A_WIDTH, D_MODEL), BRANCH_A_WIDTH ** -0.5),
        'w_br_b': nrm(25, (L, BRANCH_B_WIDTH, D_MODEL), BRANCH_B_WIDTH ** -0.5),
        'w_br_c': nrm(26, (L, BRANCH_C_WIDTH, D_MODEL), BRANCH_C_WIDTH ** -0.5),
        'w_out': nrm(27, (L, D_MODEL, D_MODEL), D_MODEL ** -0.5),
        'w_router': nrm(28, (L, D_MODEL, N_EXPERTS), D_MODEL ** -0.5),
        'w_gate': nrm(29, (L, N_EXPERTS, D_MODEL, EXPERT_HIDDEN), D_MODEL ** -0.5),
        'w_up': nrm(30, (L, N_EXPERTS, D_MODEL, EXPERT_HIDDEN), D_MODEL ** -0.5),
        'w_down': nrm(31, (L, N_EXPERTS, EXPERT_HIDDEN, D_MODEL), EXPERT_HIDDEN ** -0.5),
    }


def reference(x_prompt, x_sample, cache_swa_k, cache_swa_v, cache_mla_ckv, cache_mla_krope,
              c, c_ctx, w_mod, b_mod, norm1_g, norm2_g, w_in, pool_w, pool_scale,
              swa_q_g, swa_k_g, sink, mla_q_norm_g, w_q_b, mla_kv_norm_g, w_kv_b,
              mla_q_g, mla_k_g, w_br_a, w_br_b, w_br_c, w_out,
              w_router, w_gate, w_up, w_down):
    xp, xs = x_prompt, x_sample
    new_k, new_v, new_ckv, new_kr = [], [], [], []
    for l in range(DEPTH):
        lp = {
            'norm1_g': norm1_g[l], 'norm2_g': norm2_g[l], 'w_in': w_in[l],
            'pool_w': pool_w[l], 'pool_scale': pool_scale[l],
            'swa_q_g': swa_q_g[l], 'swa_k_g': swa_k_g[l], 'sink': sink[l],
            'mla_q_norm_g': mla_q_norm_g[l], 'w_q_b': w_q_b[l],
            'mla_kv_norm_g': mla_kv_norm_g[l], 'w_kv_b': w_kv_b[l],
            'mla_q_g': mla_q_g[l], 'mla_k_g': mla_k_g[l],
            'w_br_a': w_br_a[l], 'w_br_b': w_br_b[l], 'w_br_c': w_br_c[l], 'w_out': w_out[l],
            'w_router': w_router[l], 'w_gate': w_gate[l], 'w_up': w_up[l], 'w_down': w_down[l],
        }
        mod_ctx = (jax.nn.silu(c_ctx) @ w_mod[l] + b_mod[l])[None, None, :]
        xp, (k_l, v_l, ckv_l, kr_l) = block(xp, mod_ctx, lp)
        new_k.append(k_l)
        new_v.append(v_l)
        new_ckv.append(ckv_l)
        new_kr.append(kr_l)
        mod_lat = (jax.nn.silu(c) @ w_mod[l] + b_mod[l])[:, None, :]
        xs, _ = block(xs, mod_lat, lp,
                      (cache_swa_k[:, l], cache_swa_v[:, l], cache_mla_ckv[:, l], cache_mla_krope[:, l]))
    new_swa_k = jnp.stack(new_k, axis=1)
    new_swa_v = jnp.stack(new_v, axis=1)
    new_mla_ckv = jnp.stack(new_ckv, axis=1)
    new_mla_krope = jnp.stack(new_kr, axis=1)
    return (xp, xs, new_swa_k, new_swa_v, new_mla_ckv, new_mla_krope)
```

```python
import functools

import jax
import jax.numpy as jnp
from jax import lax
from jax.experimental import pallas as pl
from jax.experimental.pallas import tpu as pltpu

D_MODEL = 1024
DEPTH = 2
GRID_W = 64
ROPE_BASE = 10000.0
EPS = 1e-6
NEG_INF = -1e30
POOL_WINDOWS = (2, 4, 8, 16)
POOL_WIDTH = 512
SWA_HEADS = 8
SWA_HEAD_DIM = 64
SWA_WINDOW = 128
MLA_HEADS = 8
MLA_Q_RANK = 384
MLA_KV_RANK = 256
MLA_NOPE_DIM = 64
MLA_ROPE_DIM = 32
MLA_QK_DIM = MLA_NOPE_DIM + MLA_ROPE_DIM
N_EXPERTS = 16
EXPERT_HIDDEN = 1024
CAPACITY_FACTOR = 2
PAST_LEN = 256

LANES = 128
ROW_BLOCK = 256
QUERY_BLOCK = 128
POOL_PAD = 16
ZA_WIDTH = 2048
GATE_OFF = 1952
EXPERT_F_BLOCK = 512
VMEM_LIMIT = 56 * 1024 * 1024

_F32 = jnp.float32
_BF16 = jnp.bfloat16


def _bf(x):
    return x.astype(_BF16)


def _dot(a, b):
    return jnp.dot(a, b, preferred_element_type=_F32)


def _dot_nt(a, b):
    return lax.dot_general(a, b, (((1,), (1,)), ((), ())), preferred_element_type=_F32)


def _sigmoid(x):
    return 1.0 / (1.0 + jnp.exp(-x))


def _rms(x, g):
    return x * lax.rsqrt(jnp.mean(x * x, axis=-1, keepdims=True) + EPS) * g


def _lane_iota(rows):
    return lax.broadcasted_iota(jnp.int32, (rows, LANES), 1)


def _swap_lanes(x, seg):
    lane = _lane_iota(x.shape[0])
    up = pltpu.roll(x, LANES - seg, 1)
    dn = pltpu.roll(x, seg, 1)
    return jnp.where((lane & seg) == 0, up, dn)


def _rope(x, cos, sin_signed, seg):
    return x * cos + _swap_lanes(x, seg) * sin_signed


def _half_norm(x, gain):
    lo = _lane_iota(x.shape[0]) < 64
    sq = x * x
    s_lo = jnp.sum(jnp.where(lo, sq, 0.0), axis=-1, keepdims=True)
    s_hi = jnp.sum(jnp.where(lo, 0.0, sq), axis=-1, keepdims=True)
    r = jnp.where(lo, lax.rsqrt(s_lo / SWA_HEAD_DIM + EPS), lax.rsqrt(s_hi / SWA_HEAD_DIM + EPS))
    return x * r * gain


def _dup_halves(x):
    lo = _lane_iota(x.shape[0]) < 64
    sw = pltpu.roll(x, 64, 1)
    return jnp.where(lo, x, sw), jnp.where(lo, sw, x)


def _const_spec(shape):
    return pl.BlockSpec(shape, lambda *_: (0,) * len(shape))


def _params(n_axes):
    return pltpu.CompilerParams(dimension_semantics=("arbitrary",) * n_axes, vmem_limit_bytes=VMEM_LIMIT)


def _mod_kernel(cond_ref, w_ref, b_ref, o_ref):
    cnd = cond_ref[...]
    act = _bf(cnd * _sigmoid(cnd))
    o_ref[...] = _dot(act, _bf(w_ref[...])) + b_ref[...]


def _modulation(cond, w_mod, b_mod):
    n_out = w_mod.shape[-1]
    tn = 1536
    return pl.pallas_call(
        _mod_kernel,
        grid=(DEPTH, n_out // tn),
        in_specs=[
            pl.BlockSpec((8, D_MODEL), lambda l, j: (0, 0)),
            pl.BlockSpec((None, D_MODEL, tn), lambda l, j: (l, 0, j)),
            pl.BlockSpec((None, 1, tn), lambda l, j: (l, 0, j)),
        ],
        out_specs=pl.BlockSpec((None, 8, tn), lambda l, j: (l, 0, j)),
        out_shape=jax.ShapeDtypeStruct((DEPTH, 8, n_out), _F32),
        compiler_params=_params(2),
        name="modulation",
    )(cond, w_mod, b_mod.reshape(DEPTH, 1, n_out))


def _mla_kv_store(ckv, krr, wkvb_ref, mkg, rope_tabs, km_ref, vm_ref, rows):
    n_rows = ckv.shape[0]
    lo = _lane_iota(n_rows) < 64
    kv = _dot(_bf(ckv), wkvb_ref[...])
    ss_kr = jnp.sum(krr * krr, axis=-1, keepdims=True)
    krg = krr * mkg
    if rope_tabs is not None:
        krg = _rope(krg, rope_tabs[0], rope_tabs[1], 8)
    for h in range(MLA_HEADS):
        nope = jnp.where(lo, kv[:, LANES * h:LANES * (h + 1)], 0.0)
        ss = jnp.sum(nope * nope, axis=-1, keepdims=True) + ss_kr
        r = lax.rsqrt(ss / MLA_QK_DIM + EPS)
        km_ref[rows, LANES * h:LANES * (h + 1)] = _bf((nope * mkg + krg) * r)
    for j in range(MLA_HEADS // 2):
        a = kv[:, 2 * LANES * j:2 * LANES * j + LANES]
        b = kv[:, 2 * LANES * j + LANES:2 * LANES * (j + 1)]
        vm_ref[rows, LANES * j:LANES * (j + 1)] = _bf(jnp.where(lo, pltpu.roll(a, 64, 1), b))


def _tm1_kernel(is_sample, n, *refs):
    if is_sample:
        (x_ref, mod_ref, n1g_ref, qg_ref, kg_ref, qng_ref, kvng_ref, mqg_ref, mkg_ref, psc_ref,
         wa_ref, wqb_ref, wkvb_ref, pw_ref,
         cs_ref, ss_ref, cm_ref, sm_ref, ck_ref, cv_ref, cckv_ref, ckr_ref,
         qs_ref, k_ref, v_ref, qm_ref, km_ref, vm_ref, ya_ref, pad_scr) = refs
    else:
        (x_ref, mod_ref, n1g_ref, qg_ref, kg_ref, qng_ref, kvng_ref, mqg_ref, mkg_ref, psc_ref,
         wa_ref, wqb_ref, wkvb_ref, pw_ref,
         qs_ref, k_ref, v_ref, qm_ref, km_ref, vm_ref, ya_ref,
         nk_ref, nv_ref, nckv_ref, nkr_ref, pad_scr) = refs

    sh1 = mod_ref[:, 0:D_MODEL]
    sc1 = mod_ref[:, D_MODEL:2 * D_MODEL]
    qg = qg_ref[...]
    kg = kg_ref[...]
    mqg = mqg_ref[...]
    mkg = mkg_ref[...]

    zeros_pad = jnp.zeros((POOL_PAD, POOL_WIDTH), _F32)
    pad_scr[0:POOL_PAD, :] = zeros_pad
    pad_scr[POOL_PAD + n:2 * POOL_PAD + n, :] = zeros_pad

    def sub_block(j, carry):
        r0 = pl.multiple_of(j * ROW_BLOCK, ROW_BLOCK)
        rows = pl.ds(r0, ROW_BLOCK)
        x = x_ref[rows, :]
        h = _bf(_rms(x, n1g_ref[...]) * (1.0 + sc1) + sh1)
        z = _dot(h, wa_ref[...])
        pad_scr[pl.ds(POOL_PAD + r0, ROW_BLOCK), :] = z[:, 0:POOL_WIDTH]

        if is_sample:
            cs = cs_ref[rows, :]
            ss = ss_ref[rows, :]
            cm = cm_ref[rows, :]
            sm = sm_ref[rows, :]

        for c in range(4):
            qn = _half_norm(z[:, 512 + LANES * c:512 + LANES * (c + 1)], qg)
            if is_sample:
                qn = _rope(qn, cs, ss, 16)
            qs_ref[rows, LANES * c:LANES * (c + 1)] = _bf(qn * (SWA_HEAD_DIM ** -0.5))
        kn = _half_norm(z[:, 1024:1152], kg)
        vv = z[:, 1152:1280]
        if is_sample:
            kn = _rope(kn, cs, ss, 16)
        else:
            nk_ref[rows, :] = kn
            nv_ref[rows, :] = vv
        k0, k1 = _dup_halves(kn)
        k_ref[rows, 0:LANES] = _bf(k0)
        k_ref[rows, LANES:2 * LANES] = _bf(k1)
        v0, v1 = _dup_halves(vv)
        v_ref[rows, 0:LANES] = _bf(v0)
        v_ref[rows, LANES:2 * LANES] = _bf(v1)

        qcn = _bf(_rms(z[:, 1280:1664], qng_ref[...]))
        qm = _dot(qcn, wqb_ref[...])
        for hh in range(MLA_HEADS):
            t = qm[:, LANES * hh:LANES * (hh + 1)]
            tn = t * lax.rsqrt(jnp.sum(t * t, axis=-1, keepdims=True) / MLA_QK_DIM + EPS) * mqg
            if is_sample:
                tn = _rope(tn, cm, sm, 8)
            qm_ref[rows, LANES * hh:LANES * (hh + 1)] = _bf(tn * (MLA_QK_DIM ** -0.5))

        ckv = _rms(z[:, 1664:1920], kvng_ref[...])
        krs = z[:, 1920:2048]
        if not is_sample:
            nckv_ref[rows, :] = ckv
            nkr_ref[rows, :] = krs[:, 0:MLA_ROPE_DIM]
        krr = pltpu.roll(krs, 64, 1)
        _mla_kv_store(ckv, krr, wkvb_ref, mkg, (cm, sm) if is_sample else None, km_ref, vm_ref, rows)
        return carry

    lax.fori_loop(0, n // ROW_BLOCK, sub_block, 0)

    if is_sample:
        ctx_rows = pl.ds(n, PAST_LEN)
        c0, c1 = _dup_halves(ck_ref[...])
        k_ref[ctx_rows, 0:LANES] = _bf(c0)
        k_ref[ctx_rows, LANES:2 * LANES] = _bf(c1)
        c0, c1 = _dup_halves(cv_ref[...])
        v_ref[ctx_rows, 0:LANES] = _bf(c0)
        v_ref[ctx_rows, LANES:2 * LANES] = _bf(c1)
        _mla_kv_store(cckv_ref[...], ckr_ref[...], wkvb_ref, mkg, None, km_ref, vm_ref, ctx_rows)

    for c in range(n // ROW_BLOCK):
        base = POOL_PAD + ROW_BLOCK * c
        t = lax.broadcasted_iota(jnp.int32, (ROW_BLOCK, 1), 0) + ROW_BLOCK * c
        for g, w in enumerate(POOL_WINDOWS):
            half = w // 2
            cols = slice(LANES * g, LANES * (g + 1))
            acc = pad_scr[base - half:base - half + ROW_BLOCK, cols]
            for jj in range(1 - half, half):
                acc = acc + pad_scr[base + jj:base + jj + ROW_BLOCK, cols]
            u = pad_scr[base:base + ROW_BLOCK, cols]
            cnt = (jnp.minimum(t + half, n) - jnp.maximum(t - half, 0)).astype(_F32)
            dlt = _bf(acc / cnt - u)
            ya = _dot(dlt, pw_ref[g]) * psc_ref[:, cols]
            ya_ref[ROW_BLOCK * c:ROW_BLOCK * (c + 1), cols] = _bf(ya)


def _tm1(is_sample, l, x2, mod3, vecs, mats, extra):
    n = 1024 if is_sample else 256
    n_req = x2.shape[0] // n
    nk = n + PAST_LEN if is_sample else n
    mod_row = (lambda r: (r, 0, 0)) if is_sample else (lambda r: (4, 0, 0))
    in_specs = [pl.BlockSpec((n, D_MODEL), lambda r: (r, 0)),
                pl.BlockSpec((None, 1, 6 * D_MODEL), mod_row)]
    in_specs += [_const_spec(v.shape) for v in vecs]
    in_specs += [_const_spec(m.shape) for m in mats]
    if is_sample:
        tabs, caches = extra
        in_specs += [_const_spec(t.shape) for t in tabs]
        in_specs += [pl.BlockSpec((None, None, PAST_LEN, c.shape[-1]), lambda r: (r, l, 0, 0)) for c in caches]
        operands = (x2, mod3, *vecs, *mats, *tabs, *caches)
    else:
        operands = (x2, mod3, *vecs, *mats)
    tok = n_req * n
    out_shape = [jax.ShapeDtypeStruct((tok, 512), _BF16)]
    out_specs = [pl.BlockSpec((n, 512), lambda r: (r, 0))]
    for width in (256, 256):
        out_shape.append(jax.ShapeDtypeStruct((n_req, nk, width), _BF16))
        out_specs.append(pl.BlockSpec((None, nk, width), lambda r: (r, 0, 0)))
    out_shape.append(jax.ShapeDtypeStruct((tok, 1024), _BF16))
    out_specs.append(pl.BlockSpec((n, 1024), lambda r: (r, 0)))
    for width in (1024, 512):
        out_shape.append(jax.ShapeDtypeStruct((n_req, nk, width), _BF16))
        out_specs.append(pl.BlockSpec((None, nk, width), lambda r: (r, 0, 0)))
    out_shape.append(jax.ShapeDtypeStruct((tok, 512), _BF16))
    out_specs.append(pl.BlockSpec((n, 512), lambda r: (r, 0)))
    if not is_sample:
        for width in (128, 128, MLA_KV_RANK, MLA_ROPE_DIM):
            out_shape.append(jax.ShapeDtypeStruct((tok, width), _F32))
            out_specs.append(pl.BlockSpec((n, width), lambda r: (r, 0)))
    return pl.pallas_call(
        functools.partial(_tm1_kernel, is_sample, n),
        grid=(n_req,),
        in_specs=in_specs,
        out_specs=out_specs,
        out_shape=out_shape,
        scratch_shapes=[pltpu.VMEM((n + 2 * POOL_PAD, POOL_WIDTH), _F32)],
        compiler_params=_params(1),
        name="tm1_sample" if is_sample else "tm1_prompt",
    )(*operands)


def _softmax_pv(parts, sink):
    m = None
    for s, _ in parts:
        mi = jnp.max(s, axis=-1, keepdims=True)
        m = mi if m is None else jnp.maximum(m, mi)
    if sink is not None:
        m = jnp.maximum(m, sink)
    den = None
    acc = None
    for s, v in parts:
        p = jnp.exp(s - m)
        d = jnp.sum(p, axis=-1, keepdims=True)
        den = d if den is None else den + d
        o = _dot(_bf(p), v)
        acc = o if acc is None else acc + o
    if sink is not None:
        den = den + jnp.exp(sink - m)
    return acc / den


def _tm2_kernel(is_sample, *refs):
    (sink_ref, x_ref, mod_ref, n1g_ref, n2g_ref, wg_ref, qs_ref, k_ref, v_ref, qm_ref, km_ref, vm_ref,
     ya_ref, wba_ref, wbb_ref, wbc_ref, wo_ref, wr_ref,
     x1_ref, h2_ref, lg_ref, yb_scr, yc_scr) = refs
    n_own = 1024

    q_rows = QUERY_BLOCK if is_sample else ROW_BLOCK
    lo_q = _lane_iota(q_rows) < 64
    for sb in range(ROW_BLOCK // q_rows):
        rows = slice(q_rows * sb, q_rows * (sb + 1))
        if is_sample:
            blk = (pl.program_id(0) % 4) * 2 + sb
            start = jnp.clip((blk - 1) * QUERY_BLOCK, 0, n_own - 3 * QUERY_BLOCK)
            start = pl.multiple_of(start, QUERY_BLOCK)
            qpos = blk * QUERY_BLOCK + lax.broadcasted_iota(jnp.int32, (QUERY_BLOCK, 3 * QUERY_BLOCK), 0)
            kpos = start + lax.broadcasted_iota(jnp.int32, (QUERY_BLOCK, 3 * QUERY_BLOCK), 1)
            band = jnp.abs(kpos - qpos) <= SWA_WINDOW
        for j in range(SWA_HEADS // 2):
            kvh = j // 2
            kcols = slice(LANES * kvh, LANES * (kvh + 1))
            qslab = qs_ref[rows, LANES * j:LANES * (j + 1)]
            outs = []
            for half in range(2):
                keep = lo_q if half == 0 else jnp.logical_not(lo_q)
                qh = jnp.where(keep, qslab, jnp.zeros_like(qslab))
                sink = sink_ref[2 * j + half]
                if is_sample:
                    s_loc = _dot_nt(qh, k_ref[pl.ds(start, 3 * QUERY_BLOCK), kcols])
                    s_loc = jnp.where(band, s_loc, NEG_INF)
                    s_ctx = _dot_nt(qh, k_ref[n_own:n_own + PAST_LEN, kcols])
                    parts = [(s_loc, v_ref[pl.ds(start, 3 * QUERY_BLOCK), kcols]),
                             (s_ctx, v_ref[n_own:n_own + PAST_LEN, kcols])]
                else:
                    parts = [(_dot_nt(qh, k_ref[:, kcols]), v_ref[:, kcols])]
                outs.append(_softmax_pv(parts, sink))
            yb_scr[rows, LANES * j:LANES * (j + 1)] = _bf(jnp.where(lo_q, outs[0], outs[1]))

    lo = _lane_iota(ROW_BLOCK) < 64
    for j in range(MLA_HEADS // 2):
        vd = vm_ref[:, LANES * j:LANES * (j + 1)]
        outs = []
        for half in range(2):
            hh = 2 * j + half
            s = _dot_nt(qm_ref[:, LANES * hh:LANES * (hh + 1)], km_ref[:, LANES * hh:LANES * (hh + 1)])
            outs.append(_softmax_pv([(s, vd)], None))
        yc_scr[:, LANES * j:LANES * (j + 1)] = _bf(jnp.where(lo, outs[0], outs[1]))

    x = x_ref[...]
    sh1 = mod_ref[:, 0:D_MODEL]
    sc1 = mod_ref[:, D_MODEL:2 * D_MODEL]
    g1 = mod_ref[:, 2 * D_MODEL:3 * D_MODEL]
    sh2 = mod_ref[:, 3 * D_MODEL:4 * D_MODEL]
    sc2 = mod_ref[:, 4 * D_MODEL:5 * D_MODEL]
    h = _bf(_rms(x, n1g_ref[...]) * (1.0 + sc1) + sh1)
    merged = _sigmoid(_dot(h, wg_ref[:, 0:D_MODEL])) * _dot(ya_ref[...], wba_ref[...])
    merged = merged + _sigmoid(_dot(h, wg_ref[:, D_MODEL:2 * D_MODEL])) * _dot(yb_scr[...], wbb_ref[...])
    merged = merged + _sigmoid(_dot(h, wg_ref[:, 2 * D_MODEL:3 * D_MODEL])) * _dot(yc_scr[...], wbc_ref[...])
    x1 = x + g1 * _dot(_bf(merged), wo_ref[...])
    x1_ref[...] = x1
    h2 = _bf(_rms(x1, n2g_ref[...]) * (1.0 + sc2) + sh2)
    h2_ref[...] = h2
    lg_ref[...] = _dot(h2, wr_ref[...])


def _tm2(is_sample, x2, mod3, sink, vecs, wg, tm1_out, mats):
    qs, kd, vd, qm, km, vm, ya = tm1_out
    n = 1024 if is_sample else 256
    per_req = n // ROW_BLOCK
    nk = kd.shape[1]
    tok = x2.shape[0]
    mod_row = (lambda i: (i // per_req, 0, 0)) if is_sample else (lambda i: (4, 0, 0))
    row_blk = lambda width: pl.BlockSpec((ROW_BLOCK, width), lambda i: (i, 0))
    req_blk = lambda width: pl.BlockSpec((None, nk, width), lambda i: (i // per_req, 0, 0))
    in_specs = [pl.BlockSpec(memory_space=pltpu.SMEM),
                row_blk(D_MODEL),
                pl.BlockSpec((None, 1, 6 * D_MODEL), mod_row)]
    in_specs += [_const_spec(v.shape) for v in vecs]
    in_specs += [_const_spec(wg.shape), row_blk(512), req_blk(256), req_blk(256),
                 row_blk(1024), req_blk(1024), req_blk(512), row_blk(512)]
    in_specs += [_const_spec(m.shape) for m in mats]
    return pl.pallas_call(
        functools.partial(_tm2_kernel, is_sample),
        grid=(tok // ROW_BLOCK,),
        in_specs=in_specs,
        out_specs=[row_blk(D_MODEL), row_blk(D_MODEL), row_blk(LANES)],
        out_shape=[jax.ShapeDtypeStruct((tok, D_MODEL), _F32),
                   jax.ShapeDtypeStruct((tok, D_MODEL), _BF16),
                   jax.ShapeDtypeStruct((tok, LANES), _F32)],
        scratch_shapes=[pltpu.VMEM((ROW_BLOCK, 512), _BF16), pltpu.VMEM((ROW_BLOCK, 512), _BF16)],
        compiler_params=_params(1),
        name="tm2_sample" if is_sample else "tm2_prompt",
    )(sink, x2, mod3, *vecs, wg, qs, kd, vd, qm, km, vm, ya, *mats)


def _route_kernel(n, cap, lg_ref, h2_ref, xg_ref, vals_ref, rankt_ref, aff_scr, afft_scr, rank_scr, p_scr):
    lane = _lane_iota(n)
    logits = jnp.where(lane < N_EXPERTS, lg_ref[...], -jnp.inf)
    m = jnp.max(logits, axis=-1, keepdims=True)
    e = jnp.exp(logits - m)
    aff = e / jnp.sum(e, axis=-1, keepdims=True)
    aff_scr[...] = aff
    afft_scr[...] = aff.T

    rank_scr[0:N_EXPERTS, :] = jnp.zeros((N_EXPERTS, n), _F32)
    rank_scr[N_EXPERTS:LANES, :] = jnp.full((LANES - N_EXPERTS, n), float(n), _F32)

    def count_chunk(c, carry):
        m0 = pl.multiple_of(c * LANES, LANES)
        a_cols = aff_scr[pl.ds(m0, LANES), :]
        comp = m0 + lax.broadcasted_iota(jnp.int32, (LANES, n), 0)
        cand = lax.broadcasted_iota(jnp.int32, (LANES, n), 1)
        earlier = comp < cand
        for ex in range(N_EXPERTS):
            a_col = a_cols[:, ex:ex + 1]
            a_row = afft_scr[ex:ex + 1, :]
            ge = jnp.where(a_col >= a_row, 1.0, 0.0)
            gt = jnp.where(a_col > a_row, 1.0, 0.0)
            rank_scr[ex:ex + 1, :] += jnp.sum(jnp.where(earlier, ge, gt), axis=0, keepdims=True)
        return carry

    lax.fori_loop(0, n // LANES, count_chunk, 0)

    slot = lax.broadcasted_iota(jnp.int32, (cap, n), 0).astype(_F32)
    for ex in range(N_EXPERTS):
        hit = rank_scr[ex:ex + 1, :] == slot
        p_scr[cap * ex:cap * (ex + 1), :] = jnp.where(hit, 1.0, 0.0).astype(_BF16)
        vals_ref[cap * ex:cap * (ex + 1), :] = jnp.sum(
            jnp.where(hit, afft_scr[ex:ex + 1, :], 0.0), axis=1, keepdims=True)
    rankt_ref[...] = rank_scr[...].T

    chunk = 512
    for c in range(N_EXPERTS * cap // chunk):
        rows = slice(chunk * c, chunk * (c + 1))
        xg_ref[rows, :] = _bf(_dot(p_scr[rows, :], h2_ref[...]))


def _route(is_sample, logits, h2):
    n = 1024 if is_sample else 256
    cap = CAPACITY_FACTOR * n // N_EXPERTS
    n_req = logits.shape[0] // n
    slots = N_EXPERTS * cap
    return pl.pallas_call(
        functools.partial(_route_kernel, n, cap),
        grid=(n_req,),
        in_specs=[pl.BlockSpec((n, LANES), lambda r: (r, 0)),
                  pl.BlockSpec((n, D_MODEL), lambda r: (r, 0))],
        out_specs=[pl.BlockSpec((None, slots, D_MODEL), lambda r: (r, 0, 0)),
                   pl.BlockSpec((None, slots, 1), lambda r: (r, 0, 0)),
                   pl.BlockSpec((n, LANES), lambda r: (r, 0))],
        out_shape=[jax.ShapeDtypeStruct((n_req, slots, D_MODEL), _BF16),
                   jax.ShapeDtypeStruct((n_req, slots, 1), _F32),
                   jax.ShapeDtypeStruct((n_req * n, LANES), _F32)],
        scratch_shapes=[pltpu.VMEM((n, LANES), _F32), pltpu.VMEM((LANES, n), _F32),
                        pltpu.VMEM((LANES, n), _F32), pltpu.VMEM((slots, n), _BF16)],
        compiler_params=_params(1),
        name="route_sample" if is_sample else "route_prompt",
    )(logits, h2)


def _moe_kernel(xp_ref, vp_ref, xs_ref, vs_ref, wg_ref, wu_ref, wd_ref, op_ref, os_ref, accp, accs):
    f = pl.program_id(1)
    wg = _bf(wg_ref[...])
    wu = _bf(wu_ref[...])
    wd = _bf(wd_ref[...])
    for x_ref, v_ref, o_ref, acc in ((xp_ref, vp_ref, op_ref, accp), (xs_ref, vs_ref, os_ref, accs)):
        n_req, cap, _ = x_ref.shape
        xg = x_ref[...].reshape(n_req * cap, D_MODEL)
        a = _dot(xg, wg)
        u = _dot(xg, wu)
        act = _bf(a * _sigmoid(a) * u)
        part = _dot(act, wd)

        @pl.when(f == 0)
        def _():
            acc[...] = part

        @pl.when(f != 0)
        def _():
            acc[...] += part

        @pl.when(f == pl.num_programs(1) - 1)
        def _():
            scaled = acc[...] * v_ref[...].reshape(n_req * cap, 1)
            o_ref[...] = _bf(scaled).reshape(n_req, cap, D_MODEL)


def _moe(l, xg_p, vals_p, xg_s, vals_s, w_gate, w_up, w_down):
    def tok_specs(xg):
        n_req, slots, _ = xg.shape
        cap = slots // N_EXPERTS
        return (pl.BlockSpec((n_req, cap, D_MODEL), lambda e, f: (0, e, 0)),
                pl.BlockSpec((n_req, cap, 1), lambda e, f: (0, e, 0)))
    xp_spec, vp_spec = tok_specs(xg_p)
    xs_spec, vs_spec = tok_specs(xg_s)
    tf = EXPERT_F_BLOCK
    rows_p = xg_p.shape[0] * xg_p.shape[1] // N_EXPERTS
    rows_s = xg_s.shape[0] * xg_s.shape[1] // N_EXPERTS
    return pl.pallas_call(
        _moe_kernel,
        grid=(N_EXPERTS, EXPERT_HIDDEN // tf),
        in_specs=[xp_spec, vp_spec, xs_spec, vs_spec,
                  pl.BlockSpec((None, None, D_MODEL, tf), lambda e, f: (l, e, 0, f)),
                  pl.BlockSpec((None, None, D_MODEL, tf), lambda e, f: (l, e, 0, f)),
                  pl.BlockSpec((None, None, tf, D_MODEL), lambda e, f: (l, e, f, 0))],
        out_specs=[xp_spec, xs_spec],
        out_shape=[jax.ShapeDtypeStruct(xg_p.shape, _BF16), jax.ShapeDtypeStruct(xg_s.shape, _BF16)],
        scratch_shapes=[pltpu.VMEM((rows_p, D_MODEL), _F32), pltpu.VMEM((rows_s, D_MODEL), _F32)],
        compiler_params=_params(2),
        name="moe_experts",
    )(xg_p, vals_p, xg_s, vals_s, w_gate, w_up, w_down)


def _combine_kernel(n, cap, o_ref, rankt_ref, x1_ref, mod_ref, x2_ref, pt_scr):
    per = LANES // cap
    lane = _lane_iota(n)
    slot = (lane % cap).astype(_F32)
    rank_t = rankt_ref[...]
    for grp in range(N_EXPERTS // per):
        col = jnp.broadcast_to(rank_t[:, per * grp:per * grp + 1], (n, LANES))
        for k in range(1, per):
            nxt = jnp.broadcast_to(rank_t[:, per * grp + k:per * grp + k + 1], (n, LANES))
            col = jnp.where(lane >= cap * k, nxt, col)
        pt_scr[:, LANES * grp:LANES * (grp + 1)] = jnp.where(col == slot, 1.0, 0.0).astype(_BF16)
    g2 = mod_ref[:, 5 * D_MODEL:6 * D_MODEL]
    for c in range(n // ROW_BLOCK):
        rows = slice(ROW_BLOCK * c, ROW_BLOCK * (c + 1))
        y = _dot(pt_scr[rows, :], o_ref[...])
        x2_ref[rows, :] = x1_ref[rows, :] + g2 * y


def _combine(is_sample, o, rank_t, x1, mod3):
    n = 1024 if is_sample else 256
    cap = CAPACITY_FACTOR * n // N_EXPERTS
    n_req, slots, _ = o.shape
    mod_row = (lambda r: (r, 0, 0)) if is_sample else (lambda r: (4, 0, 0))
    return pl.pallas_call(
        functools.partial(_combine_kernel, n, cap),
        grid=(n_req,),
        in_specs=[pl.BlockSpec((None, slots, D_MODEL), lambda r: (r, 0, 0)),
                  pl.BlockSpec((n, LANES), lambda r: (r, 0)),
                  pl.BlockSpec((n, D_MODEL), lambda r: (r, 0)),
                  pl.BlockSpec((None, 1, 6 * D_MODEL), mod_row)],
        out_specs=pl.BlockSpec((n, D_MODEL), lambda r: (r, 0)),
        out_shape=jax.ShapeDtypeStruct(x1.shape, _F32),
        scratch_shapes=[pltpu.VMEM((n, slots), _BF16)],
        compiler_params=_params(1),
        name="combine_sample" if is_sample else "combine_prompt",
    )(o, rank_t, x1, mod3)


def _rope_tables(n):
    pos = jnp.arange(n)
    row = (pos // GRID_W).astype(_F32)
    col = (pos % GRID_W).astype(_F32)

    def axis_tab(half, p):
        freqs = ROPE_BASE ** (-jnp.arange(half, dtype=_F32) / half)
        ang = p[:, None] * freqs[None, :]
        c, s = jnp.cos(ang), jnp.sin(ang)
        return jnp.concatenate([c, c], axis=1), jnp.concatenate([-s, s], axis=1)

    cr, sr = axis_tab(16, row)
    cc, sc = axis_tab(16, col)
    cos_swa = jnp.concatenate([cr, cc, cr, cc], axis=1)
    sin_swa = jnp.concatenate([sr, sc, sr, sc], axis=1)
    cr, sr = axis_tab(8, row)
    cc, sc = axis_tab(8, col)
    ones = jnp.ones((n, 64), _F32)
    zeros = jnp.zeros((n, 64), _F32)
    cos_mla = jnp.concatenate([ones, cr, cc, ones[:, :32]], axis=1)
    sin_mla = jnp.concatenate([zeros, sr, sc, zeros[:, :32]], axis=1)
    return cos_swa, sin_swa, cos_mla, sin_mla


def kernel(x_prompt, x_sample, cache_swa_k, cache_swa_v, cache_mla_ckv, cache_mla_krope, c, c_ctx, w_mod, b_mod, norm1_g, norm2_g, w_in, pool_w, pool_scale, swa_q_g, swa_k_g, sink, mla_q_norm_g, w_q_b, mla_kv_norm_g, w_kv_b, mla_q_g, mla_k_g, w_br_a, w_br_b, w_br_c, w_out, w_router, w_gate, w_up, w_down):
    n_p, seq, _ = x_prompt.shape
    n_s, dec_seq, _ = x_sample.shape
    xp = x_prompt.reshape(n_p * seq, D_MODEL)
    xs = x_sample.reshape(n_s * dec_seq, D_MODEL)

    cond = jnp.concatenate([c, c_ctx[None, :], jnp.zeros((8 - n_s - 1, D_MODEL), _F32)], axis=0)
    mod_all = _modulation(cond, w_mod, b_mod)

    tabs = _rope_tables(dec_seq)
    ck = cache_swa_k.reshape(n_s, DEPTH, PAST_LEN, 128)
    cv = cache_swa_v.reshape(n_s, DEPTH, PAST_LEN, 128)
    ckr = jnp.pad(cache_mla_krope, ((0, 0), (0, 0), (0, 0), (64, 32)))
    caches = (ck, cv, cache_mla_ckv, ckr)

    row = lambda v: v.reshape(1, -1)
    new_k, new_v, new_ckv, new_kr = [], [], [], []
    for l in range(DEPTH):
        mod3 = mod_all[l].reshape(8, 1, 6 * D_MODEL)
        w_in_l = w_in[l]
        wa = _bf(jnp.concatenate([w_in_l[:, :1920], w_in_l[:, 1920:GATE_OFF],
                                  jnp.zeros((D_MODEL, ZA_WIDTH - GATE_OFF), _F32)], axis=1))
        wg = _bf(w_in_l[:, GATE_OFF:])
        wqb = _bf(jnp.pad(w_q_b[l].reshape(MLA_Q_RANK, MLA_HEADS, MLA_QK_DIM),
                          ((0, 0), (0, 0), (0, LANES - MLA_QK_DIM))).reshape(MLA_Q_RANK, MLA_HEADS * LANES))
        wkvb = _bf(w_kv_b[l])
        pad_head = lambda g: jnp.pad(g, (0, LANES - MLA_QK_DIM))
        vecs1 = (row(norm1_g[l]), row(jnp.tile(swa_q_g[l], 2)), row(jnp.tile(swa_k_g[l], 2)),
                 row(mla_q_norm_g[l]), row(mla_kv_norm_g[l]), row(pad_head(mla_q_g[l])),
                 row(pad_head(mla_k_g[l])), row(pool_scale[l]))
        mats1 = (wa, wqb, wkvb, _bf(pool_w[l]))
        vecs2 = (row(norm1_g[l]), row(norm2_g[l]))
        wr = _bf(jnp.pad(w_router[l], ((0, 0), (0, LANES - N_EXPERTS))))
        mats2 = (_bf(w_br_a[l]), _bf(w_br_b[l]), _bf(w_br_c[l]), _bf(w_out[l]), wr)

        out_p = _tm1(False, l, xp, mod3, vecs1, mats1, None)
        out_s = _tm1(True, l, xs, mod3, vecs1, mats1, (tabs, caches))
        new_k.append(out_p[7].reshape(n_p, seq, 2, 64))
        new_v.append(out_p[8].reshape(n_p, seq, 2, 64))
        new_ckv.append(out_p[9].reshape(n_p, seq, MLA_KV_RANK))
        new_kr.append(out_p[10].reshape(n_p, seq, MLA_ROPE_DIM))

        x1p, h2p, lgp = _tm2(False, xp, mod3, sink[l], vecs2, wg, out_p[:7], mats2)
        x1s, h2s, lgs = _tm2(True, xs, mod3, sink[l], vecs2, wg, out_s[:7], mats2)

        xg_p, vals_p, rt_p = _route(False, lgp, h2p)
        xg_s, vals_s, rt_s = _route(True, lgs, h2s)
        o_p, o_s = _moe(l, xg_p, vals_p, xg_s, vals_s, w_gate, w_up, w_down)
        xp = _combine(False, o_p, rt_p, x1p, mod3)
        xs = _combine(True, o_s, rt_s, x1s, mod3)

    return (xp.reshape(n_p, seq, D_MODEL), xs.reshape(n_s, dec_seq, D_MODEL),
            jnp.stack(new_k, axis=1), jnp.stack(new_v, axis=1),
            jnp.stack(new_ckv, axis=1), jnp.stack(new_kr, axis=1))
```

```python
import functools

import numpy as np
import jax
import jax.numpy as jnp
from jax import lax
from jax.experimental import pallas as pl
from jax.experimental.pallas import tpu as pltpu

D_MODEL = 1024
DEPTH = 2
GRID_W = 64
ROPE_BASE = 10000.0
EPS = 1e-6
NEG_INF = -1e30
POOL_WINDOWS = (2, 4, 8, 16)
POOL_WIDTH = 512
SWA_HEADS = 8
SWA_HEAD_DIM = 64
SWA_WINDOW = 128
MLA_HEADS = 8
MLA_Q_RANK = 384
MLA_KV_RANK = 256
MLA_NOPE_DIM = 64
MLA_ROPE_DIM = 32
MLA_QK_DIM = MLA_NOPE_DIM + MLA_ROPE_DIM
N_EXPERTS = 16
EXPERT_HIDDEN = 1024
CAPACITY_FACTOR = 2
PAST_LEN = 256

LANES = 128
ROW_BLOCK = 256
QUERY_BLOCK = 128
POOL_PAD = 16
ZA_WIDTH = 2048
GATE_OFF = 1952
VMEM_LIMIT = 56 * 1024 * 1024
CTX_MOD_ROW = 4

_F32 = jnp.float32
_BF16 = jnp.bfloat16


def _bf(x):
    return x.astype(_BF16)


def _dot(a, b):
    return jnp.dot(a, b, preferred_element_type=_F32)


def _dot_nt(a, b):
    return lax.dot_general(a, b, (((1,), (1,)), ((), ())), preferred_element_type=_F32)


def _sigmoid(x):
    return 1.0 / (1.0 + jnp.exp(-x))


def _rms(x, g):
    return x * lax.rsqrt(jnp.mean(x * x, axis=-1, keepdims=True) + EPS) * g


def _lane_iota(rows):
    return lax.broadcasted_iota(jnp.int32, (rows, LANES), 1)


def _swap_lanes(x, seg):
    lane = _lane_iota(x.shape[0])
    up = pltpu.roll(x, LANES - seg, 1)
    dn = pltpu.roll(x, seg, 1)
    return jnp.where((lane & seg) == 0, up, dn)


def _rope(x, cos, sin_signed, seg):
    return x * cos + _swap_lanes(x, seg) * sin_signed


def _half_norm(x, gain):
    lo = _lane_iota(x.shape[0]) < 64
    sq = x * x
    s_lo = jnp.sum(jnp.where(lo, sq, 0.0), axis=-1, keepdims=True)
    s_hi = jnp.sum(jnp.where(lo, 0.0, sq), axis=-1, keepdims=True)
    r = jnp.where(lo, lax.rsqrt(s_lo / SWA_HEAD_DIM + EPS), lax.rsqrt(s_hi / SWA_HEAD_DIM + EPS))
    return x * r * gain


def _dup_halves(x):
    lo = _lane_iota(x.shape[0]) < 64
    sw = pltpu.roll(x, 64, 1)
    return jnp.where(lo, x, sw), jnp.where(lo, sw, x)


def _const_spec(shape):
    return pl.BlockSpec(shape, lambda *_: (0,) * len(shape))


def _layer_spec(arr, l):
    return pl.BlockSpec((None,) + arr.shape[1:], lambda *_: (l,) + (0,) * (arr.ndim - 1))


def _params(n_axes):
    return pltpu.CompilerParams(dimension_semantics=("arbitrary",) * n_axes, vmem_limit_bytes=VMEM_LIMIT)


def _mod_kernel(cond_ref, w_ref, b_ref, o_ref):
    cnd = cond_ref[...]
    act = _bf(cnd * _sigmoid(cnd))
    o_ref[...] = _dot(act, _bf(w_ref[...])) + b_ref[...]


def _modulation(cond, w_mod, b_mod):
    n_out = w_mod.shape[-1]
    tn = 1536
    return pl.pallas_call(
        _mod_kernel,
        grid=(DEPTH, n_out // tn),
        in_specs=[
            pl.BlockSpec((8, D_MODEL), lambda l, j: (0, 0)),
            pl.BlockSpec((None, D_MODEL, tn), lambda l, j: (l, 0, j)),
            pl.BlockSpec((None, 1, tn), lambda l, j: (l, 0, j)),
        ],
        out_specs=pl.BlockSpec((None, 8, tn), lambda l, j: (l, 0, j)),
        out_shape=jax.ShapeDtypeStruct((DEPTH, 8, n_out), _F32),
        compiler_params=_params(2),
        name="modulation",
    )(cond, w_mod, b_mod.reshape(DEPTH, 1, n_out))


def _mla_kv_store(ckv, krr, wkvb_ref, mkg, rope_tabs, km_ref, vm_ref, rows):
    n_rows = ckv.shape[0]
    lo = _lane_iota(n_rows) < 64
    kv = _dot(_bf(ckv), wkvb_ref[...])
    ss_kr = jnp.sum(krr * krr, axis=-1, keepdims=True)
    krg = krr * mkg
    if rope_tabs is not None:
        krg = _rope(krg, rope_tabs[0], rope_tabs[1], 8)
    for h in range(MLA_HEADS):
        nope = jnp.where(lo, kv[:, LANES * h:LANES * (h + 1)], 0.0)
        ss = jnp.sum(nope * nope, axis=-1, keepdims=True) + ss_kr
        r = lax.rsqrt(ss / MLA_QK_DIM + EPS)
        km_ref[rows, LANES * h:LANES * (h + 1)] = _bf((nope * mkg + krg) * r)
    for j in range(MLA_HEADS // 2):
        a = kv[:, 2 * LANES * j:2 * LANES * j + LANES]
        b = kv[:, 2 * LANES * j + LANES:2 * LANES * (j + 1)]
        vm_ref[rows, LANES * j:LANES * (j + 1)] = _bf(jnp.where(lo, pltpu.roll(a, 64, 1), b))


def _tm1_kernel(is_sample, n, *refs):
    if is_sample:
        (x_ref, mod_ref, n1g_ref, qg_ref, kg_ref, qng_ref, kvng_ref, mqg_ref, mkg_ref, psc_ref,
         wa_ref, wqb_ref, wkvb_ref, pw_ref,
         cs_ref, ss_ref, cm_ref, sm_ref, ck_ref, cv_ref, cckv_ref, ckr_ref,
         qs_ref, k_ref, v_ref, qm_ref, km_ref, vm_ref, ya_ref, pad_scr) = refs
    else:
        (x_ref, mod_ref, n1g_ref, qg_ref, kg_ref, qng_ref, kvng_ref, mqg_ref, mkg_ref, psc_ref,
         wa_ref, wqb_ref, wkvb_ref, pw_ref,
         qs_ref, k_ref, v_ref, qm_ref, km_ref, vm_ref, ya_ref,
         nk_ref, nv_ref, nckv_ref, nkr_ref, pad_scr) = refs

    sh1 = mod_ref[:, 0:D_MODEL]
    sc1 = mod_ref[:, D_MODEL:2 * D_MODEL]
    qg = qg_ref[...]
    kg = kg_ref[...]
    mqg = mqg_ref[...]
    mkg = mkg_ref[...]

    zeros_pad = jnp.zeros((POOL_PAD, POOL_WIDTH), _F32)
    pad_scr[0:POOL_PAD, :] = zeros_pad
    pad_scr[POOL_PAD + n:2 * POOL_PAD + n, :] = zeros_pad

    def sub_block(j, carry):
        r0 = pl.multiple_of(j * ROW_BLOCK, ROW_BLOCK)
        rows = pl.ds(r0, ROW_BLOCK)
        x = x_ref[rows, :]
        h = _bf(_rms(x, n1g_ref[...]) * (1.0 + sc1) + sh1)
        z = _dot(h, wa_ref[...])
        pad_scr[pl.ds(POOL_PAD + r0, ROW_BLOCK), :] = z[:, 0:POOL_WIDTH]

        if is_sample:
            cs = cs_ref[rows, :]
            ss = ss_ref[rows, :]
            cm = cm_ref[rows, :]
            sm = sm_ref[rows, :]

        lo = _lane_iota(ROW_BLOCK) < 64
        for c in range(4):
            qn = _half_norm(z[:, 512 + LANES * c:512 + LANES * (c + 1)], qg)
            if is_sample:
                qn = _rope(qn, cs, ss, 16)
            qn = qn * (SWA_HEAD_DIM ** -0.5)
            qs_ref[2 * c, rows, :] = _bf(jnp.where(lo, qn, 0.0))
            qs_ref[2 * c + 1, rows, :] = _bf(jnp.where(lo, 0.0, qn))
        kn = _half_norm(z[:, 1024:1152], kg)
        vv = z[:, 1152:1280]
        if is_sample:
            kn = _rope(kn, cs, ss, 16)
        else:
            nk_ref[rows, :] = kn
            nv_ref[rows, :] = vv
        k0, k1 = _dup_halves(kn)
        k_ref[rows, 0:LANES] = _bf(k0)
        k_ref[rows, LANES:2 * LANES] = _bf(k1)
        v0, v1 = _dup_halves(vv)
        v_ref[rows, 0:LANES] = _bf(v0)
        v_ref[rows, LANES:2 * LANES] = _bf(v1)

        qcn = _bf(_rms(z[:, 1280:1664], qng_ref[...]))
        qm = _dot(qcn, wqb_ref[...])
        for hh in range(MLA_HEADS):
            t = qm[:, LANES * hh:LANES * (hh + 1)]
            tn = t * lax.rsqrt(jnp.sum(t * t, axis=-1, keepdims=True) / MLA_QK_DIM + EPS) * mqg
            if is_sample:
                tn = _rope(tn, cm, sm, 8)
            qm_ref[rows, LANES * hh:LANES * (hh + 1)] = _bf(tn * (MLA_QK_DIM ** -0.5))

        ckv = _rms(z[:, 1664:1920], kvng_ref[...])
        krs = jnp.where(_lane_iota(ROW_BLOCK) < MLA_ROPE_DIM, z[:, 1920:2048], 0.0)
        if not is_sample:
            nckv_ref[rows, :] = ckv
            nkr_ref[rows, :] = krs[:, 0:MLA_ROPE_DIM]
        krr = pltpu.roll(krs, 64, 1)
        _mla_kv_store(ckv, krr, wkvb_ref, mkg, (cm, sm) if is_sample else None, km_ref, vm_ref, rows)
        return carry

    lax.fori_loop(0, n // ROW_BLOCK, sub_block, 0)

    if is_sample:
        ctx_rows = pl.ds(n, PAST_LEN)
        c0, c1 = _dup_halves(ck_ref[...])
        k_ref[ctx_rows, 0:LANES] = _bf(c0)
        k_ref[ctx_rows, LANES:2 * LANES] = _bf(c1)
        c0, c1 = _dup_halves(cv_ref[...])
        v_ref[ctx_rows, 0:LANES] = _bf(c0)
        v_ref[ctx_rows, LANES:2 * LANES] = _bf(c1)
        _mla_kv_store(cckv_ref[...], ckr_ref[...], wkvb_ref, mkg, None, km_ref, vm_ref, ctx_rows)

    for c in range(n // ROW_BLOCK):
        base = POOL_PAD + ROW_BLOCK * c
        t = lax.broadcasted_iota(jnp.int32, (ROW_BLOCK, 1), 0) + ROW_BLOCK * c
        for g, w in enumerate(POOL_WINDOWS):
            half = w // 2
            cols = slice(LANES * g, LANES * (g + 1))
            acc = pad_scr[base - half:base - half + ROW_BLOCK, cols]
            for jj in range(1 - half, half):
                acc = acc + pad_scr[base + jj:base + jj + ROW_BLOCK, cols]
            u = pad_scr[base:base + ROW_BLOCK, cols]
            cnt = (jnp.minimum(t + half, n) - jnp.maximum(t - half, 0)).astype(_F32)
            dlt = _bf(acc / cnt - u)
            ya = _dot(dlt, pw_ref[g]) * psc_ref[:, cols]
            ya_ref[ROW_BLOCK * c:ROW_BLOCK * (c + 1), cols] = _bf(ya)


def _tm1(is_sample, l, x2, mod4, vecs, mats, extra):
    n = 1024 if is_sample else 256
    n_req = x2.shape[0] // n
    nk = n + PAST_LEN if is_sample else n
    mod_row = (lambda r: (l, r, 0, 0)) if is_sample else (lambda r: (l, CTX_MOD_ROW, 0, 0))
    in_specs = [pl.BlockSpec((n, D_MODEL), lambda r: (r, 0)),
                pl.BlockSpec((None, None, 1, 6 * D_MODEL), mod_row)]
    in_specs += [_layer_spec(v, l) for v in vecs]
    in_specs += [_layer_spec(m, l) for m in mats]
    if is_sample:
        tabs, caches = extra
        in_specs += [_const_spec(t.shape) for t in tabs]
        in_specs += [pl.BlockSpec((None, None, PAST_LEN, c.shape[-1]), lambda r: (r, l, 0, 0)) for c in caches]
        operands = (x2, mod4, *vecs, *mats, *tabs, *caches)
    else:
        operands = (x2, mod4, *vecs, *mats)
    tok = n_req * n
    out_shape = [jax.ShapeDtypeStruct((n_req, SWA_HEADS, n, LANES), _BF16)]
    out_specs = [pl.BlockSpec((None, SWA_HEADS, n, LANES), lambda r: (r, 0, 0, 0))]
    for width in (256, 256):
        out_shape.append(jax.ShapeDtypeStruct((n_req, nk, width), _BF16))
        out_specs.append(pl.BlockSpec((None, nk, width), lambda r: (r, 0, 0)))
    out_shape.append(jax.ShapeDtypeStruct((tok, 1024), _BF16))
    out_specs.append(pl.BlockSpec((n, 1024), lambda r: (r, 0)))
    for width in (1024, 512):
        out_shape.append(jax.ShapeDtypeStruct((n_req, nk, width), _BF16))
        out_specs.append(pl.BlockSpec((None, nk, width), lambda r: (r, 0, 0)))
    out_shape.append(jax.ShapeDtypeStruct((tok, 512), _BF16))
    out_specs.append(pl.BlockSpec((n, 512), lambda r: (r, 0)))
    if not is_sample:
        for width in (128, 128, MLA_KV_RANK, MLA_ROPE_DIM):
            out_shape.append(jax.ShapeDtypeStruct((tok, width), _F32))
            out_specs.append(pl.BlockSpec((n, width), lambda r: (r, 0)))
    return pl.pallas_call(
        functools.partial(_tm1_kernel, is_sample, n),
        grid=(n_req,),
        in_specs=in_specs,
        out_specs=out_specs,
        out_shape=out_shape,
        scratch_shapes=[pltpu.VMEM((n + 2 * POOL_PAD, POOL_WIDTH), _F32)],
        compiler_params=_params(1),
        name="tm1_sample" if is_sample else "tm1_prompt",
    )(*operands)


def _softmax_pv(parts, sink):
    m = None
    for s, _ in parts:
        mi = jnp.max(s, axis=-1, keepdims=True)
        m = mi if m is None else jnp.maximum(m, mi)
    if sink is not None:
        m = jnp.maximum(m, sink)
    den = None
    acc = None
    for s, v in parts:
        p = jnp.exp(s - m)
        d = jnp.sum(p, axis=-1, keepdims=True)
        den = d if den is None else den + d
        o = _dot(_bf(p), v)
        acc = o if acc is None else acc + o
    if sink is not None:
        den = den + jnp.exp(sink - m)
    return acc / den


def _tm2_kernel(is_sample, l, *refs):
    (sink_ref, x_ref, mod_ref, n1g_ref, n2g_ref, wg_ref, qs_ref, k_ref, v_ref, qm_ref, km_ref, vm_ref,
     ya_ref, wba_ref, wbb_ref, wbc_ref, wo_ref, wr_ref,
     x1_ref, h2_ref, lg_ref, yb_scr, yc_scr) = refs
    n_own = 1024

    q_rows = QUERY_BLOCK if is_sample else ROW_BLOCK
    grp_heads = SWA_HEADS // 2
    lo_q = _lane_iota(q_rows) < 64
    head_of_row = lax.broadcasted_iota(jnp.int32, (grp_heads * q_rows, 1), 0) // q_rows
    for sb in range(ROW_BLOCK // q_rows):
        rows = slice(q_rows * sb, q_rows * (sb + 1))
        if is_sample:
            blk = (pl.program_id(0) % 4) * 2 + sb
            start = jnp.clip((blk - 1) * QUERY_BLOCK, 0, n_own - 3 * QUERY_BLOCK)
            start = pl.multiple_of(start, QUERY_BLOCK)
            shape = (grp_heads * QUERY_BLOCK, 3 * QUERY_BLOCK)
            qpos = blk * QUERY_BLOCK + (lax.broadcasted_iota(jnp.int32, shape, 0) & (QUERY_BLOCK - 1))
            kpos = start + lax.broadcasted_iota(jnp.int32, shape, 1)
            band = jnp.abs(kpos - qpos) <= SWA_WINDOW
        for kvh in range(2):
            kcols = slice(LANES * kvh, LANES * (kvh + 1))
            q4 = qs_ref[grp_heads * kvh:grp_heads * (kvh + 1), rows, :].reshape(grp_heads * q_rows, LANES)
            sink = jnp.full((grp_heads * q_rows, 1), sink_ref[l, grp_heads * kvh], _F32)
            for hh in range(1, grp_heads):
                sink = jnp.where(head_of_row >= hh, sink_ref[l, grp_heads * kvh + hh], sink)
            if is_sample:
                s_loc = _dot_nt(q4, k_ref[pl.ds(start, 3 * QUERY_BLOCK), kcols])
                s_loc = jnp.where(band, s_loc, NEG_INF)
                s_ctx = _dot_nt(q4, k_ref[n_own:n_own + PAST_LEN, kcols])
                parts = [(s_loc, v_ref[pl.ds(start, 3 * QUERY_BLOCK), kcols]),
                         (s_ctx, v_ref[n_own:n_own + PAST_LEN, kcols])]
            else:
                parts = [(_dot_nt(q4, k_ref[:, kcols]), v_ref[:, kcols])]
            o4 = _softmax_pv(parts, sink)
            for pair in range(grp_heads // 2):
                even = o4[q_rows * 2 * pair:q_rows * (2 * pair + 1)]
                odd = o4[q_rows * (2 * pair + 1):q_rows * (2 * pair + 2)]
                col = LANES * (2 * kvh + pair)
                yb_scr[rows, col:col + LANES] = _bf(jnp.where(lo_q, even, odd))

    lo = _lane_iota(ROW_BLOCK) < 64
    for j in range(MLA_HEADS // 2):
        vd = vm_ref[:, LANES * j:LANES * (j + 1)]
        outs = []
        for half in range(2):
            hh = 2 * j + half
            s = _dot_nt(qm_ref[:, LANES * hh:LANES * (hh + 1)], km_ref[:, LANES * hh:LANES * (hh + 1)])
            outs.append(_softmax_pv([(s, vd)], None))
        yc_scr[:, LANES * j:LANES * (j + 1)] = _bf(jnp.where(lo, outs[0], outs[1]))

    x = x_ref[...]
    sh1 = mod_ref[:, 0:D_MODEL]
    sc1 = mod_ref[:, D_MODEL:2 * D_MODEL]
    g1 = mod_ref[:, 2 * D_MODEL:3 * D_MODEL]
    sh2 = mod_ref[:, 3 * D_MODEL:4 * D_MODEL]
    sc2 = mod_ref[:, 4 * D_MODEL:5 * D_MODEL]
    h = _bf(_rms(x, n1g_ref[...]) * (1.0 + sc1) + sh1)
    merged = _sigmoid(_dot(h, wg_ref[:, 0:D_MODEL])) * _dot(ya_ref[...], wba_ref[...])
    merged = merged + _sigmoid(_dot(h, wg_ref[:, D_MODEL:2 * D_MODEL])) * _dot(yb_scr[...], wbb_ref[...])
    merged = merged + _sigmoid(_dot(h, wg_ref[:, 2 * D_MODEL:3 * D_MODEL])) * _dot(yc_scr[...], wbc_ref[...])
    x1 = x + g1 * _dot(_bf(merged), wo_ref[...])
    x1_ref[...] = x1
    h2 = _bf(_rms(x1, n2g_ref[...]) * (1.0 + sc2) + sh2)
    h2_ref[...] = h2
    lg_ref[...] = _dot(h2, wr_ref[...])


def _tm2(is_sample, l, x2, mod4, sink, vecs, wg, tm1_out, mats):
    qs, kd, vd, qm, km, vm, ya = tm1_out
    n = 1024 if is_sample else 256
    per_req = n // ROW_BLOCK
    nk = kd.shape[1]
    tok = x2.shape[0]
    mod_row = (lambda i: (l, i // per_req, 0, 0)) if is_sample else (lambda i: (l, CTX_MOD_ROW, 0, 0))
    row_blk = lambda width: pl.BlockSpec((ROW_BLOCK, width), lambda i: (i, 0))
    req_blk = lambda width: pl.BlockSpec((None, nk, width), lambda i: (i // per_req, 0, 0))
    in_specs = [pl.BlockSpec(memory_space=pltpu.SMEM),
                row_blk(D_MODEL),
                pl.BlockSpec((None, None, 1, 6 * D_MODEL), mod_row)]
    in_specs += [_layer_spec(v, l) for v in vecs]
    qs_blk = pl.BlockSpec((None, SWA_HEADS, ROW_BLOCK, LANES), lambda i: (i // per_req, 0, i % per_req, 0))
    in_specs += [_layer_spec(wg, l), qs_blk, req_blk(256), req_blk(256),
                 row_blk(1024), req_blk(1024), req_blk(512), row_blk(512)]
    in_specs += [_layer_spec(m, l) for m in mats]
    return pl.pallas_call(
        functools.partial(_tm2_kernel, is_sample, l),
        grid=(tok // ROW_BLOCK,),
        in_specs=in_specs,
        out_specs=[row_blk(D_MODEL), row_blk(D_MODEL), row_blk(LANES)],
        out_shape=[jax.ShapeDtypeStruct((tok, D_MODEL), _F32),
                   jax.ShapeDtypeStruct((tok, D_MODEL), _BF16),
                   jax.ShapeDtypeStruct((tok, LANES), _F32)],
        scratch_shapes=[pltpu.VMEM((ROW_BLOCK, 512), _BF16), pltpu.VMEM((ROW_BLOCK, 512), _BF16)],
        compiler_params=_params(1),
        name="tm2_sample" if is_sample else "tm2_prompt",
    )(sink, x2, mod4, *vecs, wg, qs, kd, vd, qm, km, vm, ya, *mats)


def _sort_lanes_desc(slabs):
    rows = slabs[0].shape[0]
    n = LANES * len(slabs)
    lane = _lane_iota(rows)
    k = 2
    while k <= n:
        j = k // 2
        while j >= 1:
            out = []
            for c, x in enumerate(slabs):
                if j < LANES:
                    partner = _swap_lanes(x, j)
                    if k < LANES:
                        take_max = (((lane & j) * (k // j)) ^ (lane & k)) == 0
                    elif (LANES * c) & k == 0:
                        take_max = (lane & j) == 0
                    else:
                        take_max = (lane & j) != 0
                    out.append(jnp.where(take_max, jnp.maximum(x, partner), jnp.minimum(x, partner)))
                else:
                    partner = slabs[c ^ (j // LANES)]
                    lower = (LANES * c) & j == 0
                    desc = (LANES * c) & k == 0
                    out.append(jnp.maximum(x, partner) if lower == desc else jnp.minimum(x, partner))
            slabs = out
            j //= 2
        k *= 2
    return slabs


def _route_kernel(n, cap, grp, lg_ref, h2_ref, tri_ref, xg_ref, vals_ref, rankt_ref,
                  a_scr, code_scr, pad_scr, p_scr):
    rg = N_EXPERTS * grp
    slots = N_EXPERTS * cap
    lane = _lane_iota(grp * n)
    logits = jnp.where(lane < N_EXPERTS, lg_ref[...], -jnp.inf)
    m = jnp.max(logits, axis=-1, keepdims=True)
    e = jnp.exp(logits - m)
    aff = e / jnp.sum(e, axis=-1, keepdims=True)
    for g in range(grp):
        a_scr[N_EXPERTS * g:N_EXPERTS * (g + 1), :] = aff[n * g:n * (g + 1), :].T[0:N_EXPERTS, :]
    a = a_scr[...]

    thr = _sort_lanes_desc([a[:, LANES * c:LANES * (c + 1)] for c in range(n // LANES)])[0][:, cap - 1:cap]
    capf = float(cap)

    gt = a > thr
    eqf = jnp.where(a == thr, 1.0, 0.0)
    need = capf - jnp.sum(jnp.where(gt, 1.0, 0.0), axis=1, keepdims=True)
    eq_before = _dot(_bf(eqf), tri_ref[...])
    sel = jnp.where(gt, 1.0, jnp.where(eq_before < need, eqf, 0.0))
    pos = _dot(_bf(sel), tri_ref[...])
    code_scr[...] = jnp.where(sel > 0.5, pos, float(n))

    pad_scr[N_EXPERTS:LANES, :] = jnp.full((LANES - N_EXPERTS, n), float(n), _F32)
    slot = lax.broadcasted_iota(jnp.int32, (cap, n), 0).astype(_F32)
    chunk = min(slots, 512)
    for g in range(grp):
        for ex in range(N_EXPERTS):
            r = N_EXPERTS * g + ex
            hit = code_scr[r:r + 1, :] == slot
            p_scr[slots * g + cap * ex:slots * g + cap * (ex + 1), :] = jnp.where(hit, 1.0, 0.0).astype(_BF16)
            vals_ref[g, cap * ex:cap * (ex + 1), :] = jnp.sum(
                jnp.where(hit, a_scr[r:r + 1, :], 0.0), axis=1, keepdims=True)
        for c in range(slots // chunk):
            rows = slice(chunk * c, chunk * (c + 1))
            xg_ref[g, rows, :] = _bf(_dot(p_scr[slots * g + chunk * c:slots * g + chunk * (c + 1), :],
                                          h2_ref[n * g:n * (g + 1), :]))
        pad_scr[0:N_EXPERTS, :] = code_scr[N_EXPERTS * g:N_EXPERTS * (g + 1), :]
        rankt_ref[n * g:n * (g + 1), :] = pad_scr[...].T


def _route(is_sample, logits, h2):
    n = 1024 if is_sample else 256
    grp = 1 if is_sample else 8
    cap = CAPACITY_FACTOR * n // N_EXPERTS
    n_req = logits.shape[0] // n
    slots = N_EXPERTS * cap
    tri = jnp.asarray(np.triu(np.ones((n, n), np.float32), k=1), dtype=_BF16)
    return pl.pallas_call(
        functools.partial(_route_kernel, n, cap, grp),
        grid=(n_req // grp,),
        in_specs=[pl.BlockSpec((grp * n, LANES), lambda r: (r, 0)),
                  pl.BlockSpec((grp * n, D_MODEL), lambda r: (r, 0)),
                  _const_spec((n, n))],
        out_specs=[pl.BlockSpec((grp, slots, D_MODEL), lambda r: (r, 0, 0)),
                   pl.BlockSpec((grp, slots, 1), lambda r: (r, 0, 0)),
                   pl.BlockSpec((grp * n, LANES), lambda r: (r, 0))],
        out_shape=[jax.ShapeDtypeStruct((n_req, slots, D_MODEL), _BF16),
                   jax.ShapeDtypeStruct((n_req, slots, 1), _F32),
                   jax.ShapeDtypeStruct((n_req * n, LANES), _F32)],
        scratch_shapes=[pltpu.VMEM((N_EXPERTS * grp, n), _F32), pltpu.VMEM((N_EXPERTS * grp, n), _F32),
                        pltpu.VMEM((LANES, n), _F32), pltpu.VMEM((grp * slots, n), _BF16)],
        compiler_params=_params(1),
        name="route_sample" if is_sample else "route_prompt",
    )(logits, h2, tri)


def _moe_kernel(xp_ref, vp_ref, xs_ref, vs_ref, wg_ref, wu_ref, wd_ref, op_ref, os_ref):
    wg = _bf(wg_ref[...])
    wu = _bf(wu_ref[...])
    wd = _bf(wd_ref[...])
    for x_ref, v_ref, o_ref in ((xp_ref, vp_ref, op_ref), (xs_ref, vs_ref, os_ref)):
        n_req, cap, _ = x_ref.shape
        xg = x_ref[...].reshape(n_req * cap, D_MODEL)
        a = _dot(xg, wg)
        u = _dot(xg, wu)
        act = _bf(a * _sigmoid(a) * u)
        scaled = _dot(act, wd) * v_ref[...].reshape(n_req * cap, 1)
        o_ref[...] = _bf(scaled).reshape(n_req, cap, D_MODEL)


def _moe(l, xg_p, vals_p, xg_s, vals_s, w_gate, w_up, w_down):
    def tok_specs(xg):
        n_req, slots, _ = xg.shape
        cap = slots // N_EXPERTS
        return (pl.BlockSpec((n_req, cap, D_MODEL), lambda e: (0, e, 0)),
                pl.BlockSpec((n_req, cap, 1), lambda e: (0, e, 0)))
    xp_spec, vp_spec = tok_specs(xg_p)
    xs_spec, vs_spec = tok_specs(xg_s)
    w_spec = pl.BlockSpec((None, None, D_MODEL, EXPERT_HIDDEN), lambda e: (l, e, 0, 0))
    return pl.pallas_call(
        _moe_kernel,
        grid=(N_EXPERTS,),
        in_specs=[xp_spec, vp_spec, xs_spec, vs_spec, w_spec, w_spec, w_spec],
        out_specs=[xp_spec, xs_spec],
        out_shape=[jax.ShapeDtypeStruct(xg_p.shape, _BF16), jax.ShapeDtypeStruct(xg_s.shape, _BF16)],
        compiler_params=_params(1),
        name="moe_experts",
    )(xg_p, vals_p, xg_s, vals_s, w_gate, w_up, w_down)


def _combine_kernel(n, cap, o_ref, rankt_ref, x1_ref, mod_ref, x2_ref, pt_scr):
    per = LANES // cap
    lane = _lane_iota(n)
    slot = (lane % cap).astype(_F32)
    rank_t = rankt_ref[...]
    for grp in range(N_EXPERTS // per):
        col = jnp.broadcast_to(rank_t[:, per * grp:per * grp + 1], (n, LANES))
        for k in range(1, per):
            nxt = jnp.broadcast_to(rank_t[:, per * grp + k:per * grp + k + 1], (n, LANES))
            col = jnp.where(lane >= cap * k, nxt, col)
        pt_scr[:, LANES * grp:LANES * (grp + 1)] = jnp.where(col == slot, 1.0, 0.0).astype(_BF16)
    g2 = mod_ref[:, 5 * D_MODEL:6 * D_MODEL]
    for c in range(n // ROW_BLOCK):
        rows = slice(ROW_BLOCK * c, ROW_BLOCK * (c + 1))
        y = _dot(pt_scr[rows, :], o_ref[...])
        x2_ref[rows, :] = x1_ref[rows, :] + g2 * y


def _combine(is_sample, l, o, rank_t, x1, mod4):
    n = 1024 if is_sample else 256
    cap = CAPACITY_FACTOR * n // N_EXPERTS
    n_req, slots, _ = o.shape
    mod_row = (lambda r: (l, r, 0, 0)) if is_sample else (lambda r: (l, CTX_MOD_ROW, 0, 0))
    return pl.pallas_call(
        functools.partial(_combine_kernel, n, cap),
        grid=(n_req,),
        in_specs=[pl.BlockSpec((None, slots, D_MODEL), lambda r: (r, 0, 0)),
                  pl.BlockSpec((n, LANES), lambda r: (r, 0)),
                  pl.BlockSpec((n, D_MODEL), lambda r: (r, 0)),
                  pl.BlockSpec((None, None, 1, 6 * D_MODEL), mod_row)],
        out_specs=pl.BlockSpec((n, D_MODEL), lambda r: (r, 0)),
        out_shape=jax.ShapeDtypeStruct(x1.shape, _F32),
        scratch_shapes=[pltpu.VMEM((n, slots), _BF16)],
        compiler_params=_params(1),
        name="combine_sample" if is_sample else "combine_prompt",
    )(o, rank_t, x1, mod4)


def _rope_tables(n):
    pos = np.arange(n)
    row = (pos // GRID_W).astype(np.float32)
    col = (pos % GRID_W).astype(np.float32)

    def axis_tab(half, p):
        freqs = (ROPE_BASE ** (-np.arange(half, dtype=np.float32) / half)).astype(np.float32)
        ang = p[:, None] * freqs[None, :]
        c, s = np.cos(ang), np.sin(ang)
        return np.concatenate([c, c], axis=1), np.concatenate([-s, s], axis=1)

    cr, sr = axis_tab(16, row)
    cc, sc = axis_tab(16, col)
    cos_swa = np.concatenate([cr, cc, cr, cc], axis=1)
    sin_swa = np.concatenate([sr, sc, sr, sc], axis=1)
    cr, sr = axis_tab(8, row)
    cc, sc = axis_tab(8, col)
    ones = np.ones((n, 64), np.float32)
    zeros = np.zeros((n, 64), np.float32)
    cos_mla = np.concatenate([ones, cr, cc, ones[:, :32]], axis=1)
    sin_mla = np.concatenate([zeros, sr, sc, zeros[:, :32]], axis=1)
    return tuple(jnp.asarray(t, dtype=_F32) for t in (cos_swa, sin_swa, cos_mla, sin_mla))


def kernel(x_prompt, x_sample, cache_swa_k, cache_swa_v, cache_mla_ckv, cache_mla_krope, c, c_ctx, w_mod, b_mod, norm1_g, norm2_g, w_in, pool_w, pool_scale, swa_q_g, swa_k_g, sink, mla_q_norm_g, w_q_b, mla_kv_norm_g, w_kv_b, mla_q_g, mla_k_g, w_br_a, w_br_b, w_br_c, w_out, w_router, w_gate, w_up, w_down):
    n_p, seq, _ = x_prompt.shape
    n_s, dec_seq, _ = x_sample.shape
    xp = x_prompt.reshape(n_p * seq, D_MODEL)
    xs = x_sample.reshape(n_s * dec_seq, D_MODEL)

    cond = jnp.concatenate([c, c_ctx[None, :], jnp.zeros((8 - n_s - 1, D_MODEL), _F32)], axis=0)
    mod4 = _modulation(cond, w_mod, b_mod).reshape(DEPTH, 8, 1, 6 * D_MODEL)

    tabs = _rope_tables(dec_seq)
    ck = cache_swa_k.reshape(n_s, DEPTH, PAST_LEN, 128)
    cv = cache_swa_v.reshape(n_s, DEPTH, PAST_LEN, 128)
    ckr = jnp.pad(cache_mla_krope, ((0, 0), (0, 0), (0, 0), (64, 32)))
    caches = (ck, cv, cache_mla_ckv, ckr)

    vec = lambda v: v.reshape(DEPTH, 1, -1)
    pad_head = lambda g: jnp.pad(g, ((0, 0), (0, LANES - MLA_QK_DIM)))
    wqb = jnp.pad(w_q_b.reshape(DEPTH, MLA_Q_RANK, MLA_HEADS, MLA_QK_DIM),
                  ((0, 0), (0, 0), (0, 0), (0, LANES - MLA_QK_DIM)))
    vecs1 = (vec(norm1_g), vec(jnp.tile(swa_q_g, (1, 2))), vec(jnp.tile(swa_k_g, (1, 2))),
             vec(mla_q_norm_g), vec(mla_kv_norm_g), vec(pad_head(mla_q_g)), vec(pad_head(mla_k_g)),
             vec(pool_scale))
    mats1 = (_bf(w_in[:, :, :ZA_WIDTH]), _bf(wqb.reshape(DEPTH, MLA_Q_RANK, MLA_HEADS * LANES)),
             _bf(w_kv_b), _bf(pool_w))
    vecs2 = (vec(norm1_g), vec(norm2_g))
    wg = _bf(w_in[:, :, GATE_OFF:])
    mats2 = (_bf(w_br_a), _bf(w_br_b), _bf(w_br_c), _bf(w_out),
             _bf(jnp.pad(w_router, ((0, 0), (0, 0), (0, LANES - N_EXPERTS)))))

    new_k, new_v, new_ckv, new_kr = [], [], [], []
    for l in range(DEPTH):
        out_p = _tm1(False, l, xp, mod4, vecs1, mats1, None)
        out_s = _tm1(True, l, xs, mod4, vecs1, mats1, (tabs, caches))
        new_k.append(out_p[7].reshape(n_p, seq, 2, 64))
        new_v.append(out_p[8].reshape(n_p, seq, 2, 64))
        new_ckv.append(out_p[9].reshape(n_p, seq, MLA_KV_RANK))
        new_kr.append(out_p[10].reshape(n_p, seq, MLA_ROPE_DIM))

        x1p, h2p, lgp = _tm2(False, l, xp, mod4, sink, vecs2, wg, out_p[:7], mats2)
        x1s, h2s, lgs = _tm2(True, l, xs, mod4, sink, vecs2, wg, out_s[:7], mats2)

        xg_p, vals_p, rt_p = _route(False, lgp, h2p)
        xg_s, vals_s, rt_s = _route(True, lgs, h2s)
        o_p, o_s = _moe(l, xg_p, vals_p, xg_s, vals_s, w_gate, w_up, w_down)
        xp = _combine(False, l, o_p, rt_p, x1p, mod4)
        xs = _combine(True, l, o_s, rt_s, x1s, mod4)

    return (xp.reshape(n_p, seq, D_MODEL), xs.reshape(n_s, dec_seq, D_MODEL),
            jnp.stack(new_k, axis=1), jnp.stack(new_v, axis=1),
            jnp.stack(new_ckv, axis=1), jnp.stack(new_kr, axis=1))
```

```python
import functools

import numpy as np
import jax
import jax.numpy as jnp
from jax import lax
from jax.experimental import pallas as pl
from jax.experimental.pallas import tpu as pltpu

D_MODEL = 1024
DEPTH = 2
GRID_W = 64
ROPE_BASE = 10000.0
EPS = 1e-6
NEG_INF = -1e30
POOL_WINDOWS = (2, 4, 8, 16)
POOL_WIDTH = 512
SWA_HEADS = 8
SWA_HEAD_DIM = 64
SWA_WINDOW = 128
MLA_HEADS = 8
MLA_Q_RANK = 384
MLA_KV_RANK = 256
MLA_NOPE_DIM = 64
MLA_ROPE_DIM = 32
MLA_QK_DIM = MLA_NOPE_DIM + MLA_ROPE_DIM
N_EXPERTS = 16
EXPERT_HIDDEN = 1024
CAPACITY_FACTOR = 2
PAST_LEN = 256

LANES = 128
ROW_BLOCK = 256
QUERY_BLOCK = 128
POOL_PAD = 16
ZA_WIDTH = 2048
GATE_OFF = 1952
VMEM_LIMIT = 56 * 1024 * 1024
CTX_MOD_ROW = 4
LOG2E = 1.4426950408889634

_F32 = jnp.float32
_BF16 = jnp.bfloat16


def _bf(x):
    return x.astype(_BF16)


def _dot(a, b):
    return jnp.dot(a, b, preferred_element_type=_F32)


def _dot_nt(a, b):
    return lax.dot_general(a, b, (((1,), (1,)), ((), ())), preferred_element_type=_F32)


def _sigmoid(x):
    return 1.0 / (1.0 + jnp.exp(-x))


def _rms(x, g):
    return x * lax.rsqrt(jnp.mean(x * x, axis=-1, keepdims=True) + EPS) * g


def _mod_norm(x, gain_mod, shift):
    return _bf(x * lax.rsqrt(jnp.mean(x * x, axis=-1, keepdims=True) + EPS) * gain_mod + shift)


def _lane_iota(rows):
    return lax.broadcasted_iota(jnp.int32, (rows, LANES), 1)


def _swap_lanes(x, seg):
    lane = _lane_iota(x.shape[0])
    up = pltpu.roll(x, LANES - seg, 1)
    dn = pltpu.roll(x, seg, 1)
    return jnp.where((lane & seg) == 0, up, dn)


def _rope(x, cos, sin_signed, seg):
    return x * cos + _swap_lanes(x, seg) * sin_signed


def _half_norm(x, gain):
    lo = _lane_iota(x.shape[0]) < 64
    sq = x * x
    s_lo = jnp.sum(jnp.where(lo, sq, 0.0), axis=-1, keepdims=True)
    s_hi = jnp.sum(jnp.where(lo, 0.0, sq), axis=-1, keepdims=True)
    r = jnp.where(lo, lax.rsqrt(s_lo / SWA_HEAD_DIM + EPS), lax.rsqrt(s_hi / SWA_HEAD_DIM + EPS))
    return x * r * gain


def _dup_halves(x):
    lo = _lane_iota(x.shape[0]) < 64
    sw = pltpu.roll(x, 64, 1)
    return jnp.where(lo, x, sw), jnp.where(lo, sw, x)


def _const_spec(shape):
    return pl.BlockSpec(shape, lambda *_: (0,) * len(shape))


def _layer_spec(arr, l):
    return pl.BlockSpec((None,) + arr.shape[1:], lambda *_: (l,) + (0,) * (arr.ndim - 1))


def _params(n_axes):
    return pltpu.CompilerParams(dimension_semantics=("arbitrary",) * n_axes, vmem_limit_bytes=VMEM_LIMIT)


def _mod_kernel(cond_ref, w_ref, b_ref, o_ref):
    cnd = cond_ref[...]
    act = _bf(cnd * _sigmoid(cnd))
    o_ref[...] = _dot(act, _bf(w_ref[...])) + b_ref[...]


def _modulation(cond, w_mod, b_mod):
    n_out = w_mod.shape[-1]
    tn = 1536
    return pl.pallas_call(
        _mod_kernel,
        grid=(DEPTH, n_out // tn),
        in_specs=[
            pl.BlockSpec((8, D_MODEL), lambda l, j: (0, 0)),
            pl.BlockSpec((None, D_MODEL, tn), lambda l, j: (l, 0, j)),
            pl.BlockSpec((None, 1, tn), lambda l, j: (l, 0, j)),
        ],
        out_specs=pl.BlockSpec((None, 8, tn), lambda l, j: (l, 0, j)),
        out_shape=jax.ShapeDtypeStruct((DEPTH, 8, n_out), _F32),
        compiler_params=_params(2),
        name="modulation",
    )(cond, w_mod, b_mod.reshape(DEPTH, 1, n_out))


def _mla_kv_store(ckv, krr, wkvb_ref, mkg, rope_tabs, km_ref, vm_ref, rows):
    n_rows = ckv.shape[0]
    lo = _lane_iota(n_rows) < 64
    kv = _dot(_bf(ckv), wkvb_ref[...])
    ss_kr = jnp.sum(krr * krr, axis=-1, keepdims=True)
    krg = krr * mkg
    if rope_tabs is not None:
        krg = _rope(krg, rope_tabs[0], rope_tabs[1], 8)
    for h in range(MLA_HEADS):
        nope = jnp.where(lo, kv[:, LANES * h:LANES * (h + 1)], 0.0)
        ss = jnp.sum(nope * nope, axis=-1, keepdims=True) + ss_kr
        r = lax.rsqrt(ss / MLA_QK_DIM + EPS)
        km_ref[rows, LANES * h:LANES * (h + 1)] = _bf((nope * mkg + krg) * r)
    for j in range(MLA_HEADS // 2):
        a = kv[:, 2 * LANES * j:2 * LANES * j + LANES]
        b = kv[:, 2 * LANES * j + LANES:2 * LANES * (j + 1)]
        vm_ref[rows, LANES * j:LANES * (j + 1)] = _bf(jnp.where(lo, pltpu.roll(a, 64, 1), b))


def _tm1_kernel(is_sample, n, *refs):
    if is_sample:
        (x_ref, mod_ref, n1g_ref, qg_ref, kg_ref, qng_ref, kvng_ref, mqg_ref, mkg_ref, psc_ref,
         wa_ref, wqb_ref, wkvb_ref, pw_ref,
         cs_ref, ss_ref, cm_ref, sm_ref, ck_ref, cv_ref, cckv_ref, ckr_ref,
         qs_ref, k_ref, v_ref, qm_ref, km_ref, vm_ref, ya_ref, pad_scr) = refs
    else:
        (x_ref, mod_ref, n1g_ref, qg_ref, kg_ref, qng_ref, kvng_ref, mqg_ref, mkg_ref, psc_ref,
         wa_ref, wqb_ref, wkvb_ref, pw_ref,
         qs_ref, k_ref, v_ref, qm_ref, km_ref, vm_ref, ya_ref,
         nk_ref, nv_ref, nckv_ref, nkr_ref, pad_scr) = refs

    sh1 = mod_ref[:, 0:D_MODEL]
    gm1 = n1g_ref[...] * (1.0 + mod_ref[:, D_MODEL:2 * D_MODEL])
    qg = qg_ref[...] * (SWA_HEAD_DIM ** -0.5 * LOG2E)
    kg = kg_ref[...]
    mqg = mqg_ref[...] * (MLA_QK_DIM ** -0.5 * LOG2E)
    mkg = mkg_ref[...]

    zeros_pad = jnp.zeros((POOL_PAD, POOL_WIDTH), _F32)
    pad_scr[0:POOL_PAD, :] = zeros_pad
    pad_scr[POOL_PAD + n:2 * POOL_PAD + n, :] = zeros_pad

    def in_proj(j):
        x = x_ref[ROW_BLOCK * j:ROW_BLOCK * (j + 1), :]
        return _dot(_mod_norm(x, gm1, sh1), wa_ref[...])

    def sub_block(j, z):
        r0 = ROW_BLOCK * j
        rows = slice(r0, r0 + ROW_BLOCK)
        pad_scr[POOL_PAD + r0:POOL_PAD + r0 + ROW_BLOCK, :] = z[:, 0:POOL_WIDTH]

        if is_sample:
            cs = cs_ref[rows, :]
            ss = ss_ref[rows, :]
            cm = cm_ref[rows, :]
            sm = sm_ref[rows, :]

        lo = _lane_iota(ROW_BLOCK) < 64
        for c in range(4):
            qn = _half_norm(z[:, 512 + LANES * c:512 + LANES * (c + 1)], qg)
            if is_sample:
                qn = _rope(qn, cs, ss, 16)
            qs_ref[2 * c, rows, :] = _bf(jnp.where(lo, qn, 0.0))
            qs_ref[2 * c + 1, rows, :] = _bf(jnp.where(lo, 0.0, qn))
        kn = _half_norm(z[:, 1024:1152], kg)
        vv = z[:, 1152:1280]
        if is_sample:
            kn = _rope(kn, cs, ss, 16)
        else:
            nk_ref[rows, :] = kn
            nv_ref[rows, :] = vv
        k0, k1 = _dup_halves(kn)
        k_ref[rows, 0:LANES] = _bf(k0)
        k_ref[rows, LANES:2 * LANES] = _bf(k1)
        v0, v1 = _dup_halves(vv)
        v_ref[rows, 0:LANES] = _bf(v0)
        v_ref[rows, LANES:2 * LANES] = _bf(v1)

        qcn = _bf(_rms(z[:, 1280:1664], qng_ref[...]))
        qm = _dot(qcn, wqb_ref[...])
        for hh in range(MLA_HEADS):
            t = qm[:, LANES * hh:LANES * (hh + 1)]
            tn = t * lax.rsqrt(jnp.sum(t * t, axis=-1, keepdims=True) / MLA_QK_DIM + EPS) * mqg
            if is_sample:
                tn = _rope(tn, cm, sm, 8)
            qm_ref[rows, LANES * hh:LANES * (hh + 1)] = _bf(tn)

        ckv = _rms(z[:, 1664:1920], kvng_ref[...])
        krs = jnp.where(_lane_iota(ROW_BLOCK) < MLA_ROPE_DIM, z[:, 1920:2048], 0.0)
        if not is_sample:
            nckv_ref[rows, :] = ckv
            nkr_ref[rows, :] = krs[:, 0:MLA_ROPE_DIM]
        krr = pltpu.roll(krs, 64, 1)
        _mla_kv_store(ckv, krr, wkvb_ref, mkg, (cm, sm) if is_sample else None, km_ref, vm_ref, rows)

    n_blocks = n // ROW_BLOCK
    z_next = in_proj(0)
    for j in range(n_blocks):
        z_cur = z_next
        if j + 1 < n_blocks:
            z_next = in_proj(j + 1)
        sub_block(j, z_cur)

    if is_sample:
        ctx_rows = slice(n, n + PAST_LEN)
        c0, c1 = _dup_halves(ck_ref[...])
        k_ref[ctx_rows, 0:LANES] = _bf(c0)
        k_ref[ctx_rows, LANES:2 * LANES] = _bf(c1)
        c0, c1 = _dup_halves(cv_ref[...])
        v_ref[ctx_rows, 0:LANES] = _bf(c0)
        v_ref[ctx_rows, LANES:2 * LANES] = _bf(c1)
        _mla_kv_store(cckv_ref[...], ckr_ref[...], wkvb_ref, mkg, None, km_ref, vm_ref, ctx_rows)

    for c in range(n // ROW_BLOCK):
        base = POOL_PAD + ROW_BLOCK * c
        t = lax.broadcasted_iota(jnp.int32, (ROW_BLOCK, 1), 0) + ROW_BLOCK * c
        for g, w in enumerate(POOL_WINDOWS):
            half = w // 2
            cols = slice(LANES * g, LANES * (g + 1))
            acc = pad_scr[base - half:base - half + ROW_BLOCK, cols]
            for jj in range(1 - half, half):
                acc = acc + pad_scr[base + jj:base + jj + ROW_BLOCK, cols]
            u = pad_scr[base:base + ROW_BLOCK, cols]
            cnt = (jnp.minimum(t + half, n) - jnp.maximum(t - half, 0)).astype(_F32)
            dlt = _bf(acc / cnt - u)
            ya = _dot(dlt, pw_ref[g]) * psc_ref[:, cols]
            ya_ref[ROW_BLOCK * c:ROW_BLOCK * (c + 1), cols] = _bf(ya)


def _tm1(is_sample, l, x2, mod4, vecs, mats, extra):
    n = 1024 if is_sample else 256
    n_req = x2.shape[0] // n
    nk = n + PAST_LEN if is_sample else n
    mod_row = (lambda r: (l, r, 0, 0)) if is_sample else (lambda r: (l, CTX_MOD_ROW, 0, 0))
    in_specs = [pl.BlockSpec((n, D_MODEL), lambda r: (r, 0)),
                pl.BlockSpec((None, None, 1, 6 * D_MODEL), mod_row)]
    in_specs += [_layer_spec(v, l) for v in vecs]
    in_specs += [_layer_spec(m, l) for m in mats]
    if is_sample:
        tabs, caches = extra
        in_specs += [_const_spec(t.shape) for t in tabs]
        in_specs += [pl.BlockSpec((None, None, PAST_LEN, c.shape[-1]), lambda r: (r, l, 0, 0)) for c in caches]
        operands = (x2, mod4, *vecs, *mats, *tabs, *caches)
    else:
        operands = (x2, mod4, *vecs, *mats)
    tok = n_req * n
    out_shape = [jax.ShapeDtypeStruct((n_req, SWA_HEADS, n, LANES), _BF16)]
    out_specs = [pl.BlockSpec((None, SWA_HEADS, n, LANES), lambda r: (r, 0, 0, 0))]
    for width in (256, 256):
        out_shape.append(jax.ShapeDtypeStruct((n_req, nk, width), _BF16))
        out_specs.append(pl.BlockSpec((None, nk, width), lambda r: (r, 0, 0)))
    out_shape.append(jax.ShapeDtypeStruct((tok, 1024), _BF16))
    out_specs.append(pl.BlockSpec((n, 1024), lambda r: (r, 0)))
    for width in (1024, 512):
        out_shape.append(jax.ShapeDtypeStruct((n_req, nk, width), _BF16))
        out_specs.append(pl.BlockSpec((None, nk, width), lambda r: (r, 0, 0)))
    out_shape.append(jax.ShapeDtypeStruct((tok, 512), _BF16))
    out_specs.append(pl.BlockSpec((n, 512), lambda r: (r, 0)))
    if not is_sample:
        for width in (128, 128, MLA_KV_RANK, MLA_ROPE_DIM):
            out_shape.append(jax.ShapeDtypeStruct((tok, width), _F32))
            out_specs.append(pl.BlockSpec((n, width), lambda r: (r, 0)))
    return pl.pallas_call(
        functools.partial(_tm1_kernel, is_sample, n),
        grid=(n_req,),
        in_specs=in_specs,
        out_specs=out_specs,
        out_shape=out_shape,
        scratch_shapes=[pltpu.VMEM((n + 2 * POOL_PAD, POOL_WIDTH), _F32)],
        compiler_params=_params(1),
        name="tm1_sample" if is_sample else "tm1_prompt",
    )(*operands)


def _softmax_pv(parts, sink):
    m = None
    for s, _ in parts:
        mi = jnp.max(s, axis=-1, keepdims=True)
        m = mi if m is None else jnp.maximum(m, mi)
    if sink is not None:
        m = jnp.maximum(m, sink)
    den = None
    acc = None
    for s, v in parts:
        p = jnp.exp2(s - m)
        d = jnp.sum(p, axis=-1, keepdims=True)
        den = d if den is None else den + d
        o = _dot(_bf(p), v)
        acc = o if acc is None else acc + o
    if sink is not None:
        den = den + jnp.exp2(sink - m)
    return acc / den


def _tm2_kernel(is_sample, l, *refs):
    (sink_ref, x_ref, mod_ref, n1g_ref, n2g_ref, wg_ref, qs_ref, k_ref, v_ref, qm_ref, km_ref, vm_ref,
     ya_ref, wba_ref, wbb_ref, wbc_ref, wo_ref, wr_ref,
     x1_ref, h2_ref, lg_ref, yb_scr, yc_scr) = refs
    n_own = 1024

    q_rows = QUERY_BLOCK if is_sample else ROW_BLOCK
    grp_heads = SWA_HEADS // 2
    lo_q = _lane_iota(q_rows) < 64
    head_of_row = lax.broadcasted_iota(jnp.int32, (grp_heads * q_rows, 1), 0) // q_rows

    def swa_chain_inputs(sb, kvh):
        rows = slice(q_rows * sb, q_rows * (sb + 1))
        kcols = slice(LANES * kvh, LANES * (kvh + 1))
        q4 = qs_ref[grp_heads * kvh:grp_heads * (kvh + 1), rows, :].reshape(grp_heads * q_rows, LANES)
        sink = jnp.full((grp_heads * q_rows, 1), sink_ref[l, grp_heads * kvh] * LOG2E, _F32)
        for hh in range(1, grp_heads):
            sink = jnp.where(head_of_row >= hh, sink_ref[l, grp_heads * kvh + hh] * LOG2E, sink)
        if not is_sample:
            return [(_dot_nt(q4, k_ref[:, kcols]), v_ref[:, kcols])], sink
        blk = (pl.program_id(0) % 4) * 2 + sb
        start = jnp.clip((blk - 1) * QUERY_BLOCK, 0, n_own - 3 * QUERY_BLOCK)
        start = pl.multiple_of(start, QUERY_BLOCK)
        shape = (grp_heads * QUERY_BLOCK, 3 * QUERY_BLOCK)
        qpos = blk * QUERY_BLOCK + (lax.broadcasted_iota(jnp.int32, shape, 0) & (QUERY_BLOCK - 1))
        kpos = start + lax.broadcasted_iota(jnp.int32, shape, 1)
        band = jnp.abs(kpos - qpos) <= SWA_WINDOW
        s_loc = _dot_nt(q4, k_ref[pl.ds(start, 3 * QUERY_BLOCK), kcols])
        s_loc = jnp.where(band, s_loc, NEG_INF)
        s_ctx = _dot_nt(q4, k_ref[n_own:n_own + PAST_LEN, kcols])
        return [(s_loc, v_ref[pl.ds(start, 3 * QUERY_BLOCK), kcols]),
                (s_ctx, v_ref[n_own:n_own + PAST_LEN, kcols])], sink

    chains = [(sb, kvh) for sb in range(ROW_BLOCK // q_rows) for kvh in range(2)]
    nxt = swa_chain_inputs(*chains[0])
    for ci, (sb, kvh) in enumerate(chains):
        parts, sink = nxt
        if ci + 1 < len(chains):
            nxt = swa_chain_inputs(*chains[ci + 1])
        rows = slice(q_rows * sb, q_rows * (sb + 1))
        o4 = _softmax_pv(parts, sink)
        for pair in range(grp_heads // 2):
            even = o4[q_rows * 2 * pair:q_rows * (2 * pair + 1)]
            odd = o4[q_rows * (2 * pair + 1):q_rows * (2 * pair + 2)]
            col = LANES * (2 * kvh + pair)
            yb_scr[rows, col:col + LANES] = _bf(jnp.where(lo_q, even, odd))

    lo = _lane_iota(ROW_BLOCK) < 64

    def mla_scores(hh):
        return _dot_nt(qm_ref[:, LANES * hh:LANES * (hh + 1)], km_ref[:, LANES * hh:LANES * (hh + 1)])

    lookahead = is_sample
    s_next = mla_scores(0)
    outs = []
    for hh in range(MLA_HEADS):
        s_cur = s_next if lookahead or hh == 0 else mla_scores(hh)
        if lookahead and hh + 1 < MLA_HEADS:
            s_next = mla_scores(hh + 1)
        j = hh // 2
        outs.append(_softmax_pv([(s_cur, vm_ref[:, LANES * j:LANES * (j + 1)])], None))
        if hh % 2 == 1:
            yc_scr[:, LANES * j:LANES * (j + 1)] = _bf(jnp.where(lo, outs[0], outs[1]))
            outs = []

    x = x_ref[...]
    sh1 = mod_ref[:, 0:D_MODEL]
    gm1 = n1g_ref[...] * (1.0 + mod_ref[:, D_MODEL:2 * D_MODEL])
    g1 = mod_ref[:, 2 * D_MODEL:3 * D_MODEL]
    sh2 = mod_ref[:, 3 * D_MODEL:4 * D_MODEL]
    gm2 = n2g_ref[...] * (1.0 + mod_ref[:, 4 * D_MODEL:5 * D_MODEL])
    h = _mod_norm(x, gm1, sh1)
    merged = _sigmoid(_dot(h, wg_ref[:, 0:D_MODEL])) * _dot(ya_ref[...], wba_ref[...])
    merged = merged + _sigmoid(_dot(h, wg_ref[:, D_MODEL:2 * D_MODEL])) * _dot(yb_scr[...], wbb_ref[...])
    merged = merged + _sigmoid(_dot(h, wg_ref[:, 2 * D_MODEL:3 * D_MODEL])) * _dot(yc_scr[...], wbc_ref[...])
    x1 = x + g1 * _dot(_bf(merged), wo_ref[...])
    x1_ref[...] = x1
    h2 = _mod_norm(x1, gm2, sh2)
    h2_ref[...] = h2
    lg_ref[...] = _dot(h2, wr_ref[...])


def _tm2(is_sample, l, x2, mod4, sink, vecs, wg, tm1_out, mats):
    qs, kd, vd, qm, km, vm, ya = tm1_out
    n = 1024 if is_sample else 256
    per_req = n // ROW_BLOCK
    nk = kd.shape[1]
    tok = x2.shape[0]
    mod_row = (lambda i: (l, i // per_req, 0, 0)) if is_sample else (lambda i: (l, CTX_MOD_ROW, 0, 0))
    row_blk = lambda width: pl.BlockSpec((ROW_BLOCK, width), lambda i: (i, 0))
    req_blk = lambda width: pl.BlockSpec((None, nk, width), lambda i: (i // per_req, 0, 0))
    in_specs = [pl.BlockSpec(memory_space=pltpu.SMEM),
                row_blk(D_MODEL),
                pl.BlockSpec((None, None, 1, 6 * D_MODEL), mod_row)]
    in_specs += [_layer_spec(v, l) for v in vecs]
    qs_blk = pl.BlockSpec((None, SWA_HEADS, ROW_BLOCK, LANES), lambda i: (i // per_req, 0, i % per_req, 0))
    in_specs += [_layer_spec(wg, l), qs_blk, req_blk(256), req_blk(256),
                 row_blk(1024), req_blk(1024), req_blk(512), row_blk(512)]
    in_specs += [_layer_spec(m, l) for m in mats]
    return pl.pallas_call(
        functools.partial(_tm2_kernel, is_sample, l),
        grid=(tok // ROW_BLOCK,),
        in_specs=in_specs,
        out_specs=[row_blk(D_MODEL), row_blk(D_MODEL), row_blk(LANES)],
        out_shape=[jax.ShapeDtypeStruct((tok, D_MODEL), _F32),
                   jax.ShapeDtypeStruct((tok, D_MODEL), _BF16),
                   jax.ShapeDtypeStruct((tok, LANES), _F32)],
        scratch_shapes=[pltpu.VMEM((ROW_BLOCK, 512), _BF16), pltpu.VMEM((ROW_BLOCK, 512), _BF16)],
        compiler_params=_params(1),
        name="tm2_sample" if is_sample else "tm2_prompt",
    )(sink, x2, mod4, *vecs, wg, qs, kd, vd, qm, km, vm, ya, *mats)


def _sort_lanes_desc(slabs):
    rows = slabs[0].shape[0]
    n = LANES * len(slabs)
    lane = _lane_iota(rows)
    k = 2
    while k <= n:
        j = k // 2
        while j >= 1:
            out = []
            for c, x in enumerate(slabs):
                if j < LANES:
                    partner = _swap_lanes(x, j)
                    if k < LANES:
                        take_max = (((lane & j) * (k // j)) ^ (lane & k)) == 0
                    elif (LANES * c) & k == 0:
                        take_max = (lane & j) == 0
                    else:
                        take_max = (lane & j) != 0
                    out.append(jnp.where(take_max, jnp.maximum(x, partner), jnp.minimum(x, partner)))
                else:
                    partner = slabs[c ^ (j // LANES)]
                    lower = (LANES * c) & j == 0
                    desc = (LANES * c) & k == 0
                    out.append(jnp.maximum(x, partner) if lower == desc else jnp.minimum(x, partner))
            slabs = out
            j //= 2
        k *= 2
    return slabs


def _route_kernel(n, cap, grp, lg_ref, h2_ref, tri_ref, xg_ref, vals_ref, rankt_ref,
                  a_scr, code_scr, pad_scr, p_scr):
    rg = N_EXPERTS * grp
    slots = N_EXPERTS * cap
    lane = _lane_iota(grp * n)
    logits = jnp.where(lane < N_EXPERTS, lg_ref[...], -jnp.inf)
    m = jnp.max(logits, axis=-1, keepdims=True)
    e = jnp.exp(logits - m)
    aff = e / jnp.sum(e, axis=-1, keepdims=True)
    for g in range(grp):
        a_scr[N_EXPERTS * g:N_EXPERTS * (g + 1), :] = aff[n * g:n * (g + 1), :].T[0:N_EXPERTS, :]
    a = a_scr[...]

    thr = _sort_lanes_desc([a[:, LANES * c:LANES * (c + 1)] for c in range(n // LANES)])[0][:, cap - 1:cap]
    capf = float(cap)

    gt = a > thr
    eqf = jnp.where(a == thr, 1.0, 0.0)
    need = capf - jnp.sum(jnp.where(gt, 1.0, 0.0), axis=1, keepdims=True)
    eq_before = _dot(_bf(eqf), tri_ref[...])
    sel = jnp.where(gt, 1.0, jnp.where(eq_before < need, eqf, 0.0))
    pos = _dot(_bf(sel), tri_ref[...])
    code_scr[...] = jnp.where(sel > 0.5, pos, float(n))

    pad_scr[N_EXPERTS:LANES, :] = jnp.full((LANES - N_EXPERTS, n), float(n), _F32)
    slot = lax.broadcasted_iota(jnp.int32, (cap, n), 0).astype(_F32)
    chunk = min(slots, 512)
    for g in range(grp):
        for ex in range(N_EXPERTS):
            r = N_EXPERTS * g + ex
            hit = code_scr[r:r + 1, :] == slot
            p_scr[slots * g + cap * ex:slots * g + cap * (ex + 1), :] = jnp.where(hit, 1.0, 0.0).astype(_BF16)
            vals_ref[g, cap * ex:cap * (ex + 1), :] = jnp.sum(
                jnp.where(hit, a_scr[r:r + 1, :], 0.0), axis=1, keepdims=True)
        for c in range(slots // chunk):
            rows = slice(chunk * c, chunk * (c + 1))
            xg_ref[g, rows, :] = _bf(_dot(p_scr[slots * g + chunk * c:slots * g + chunk * (c + 1), :],
                                          h2_ref[n * g:n * (g + 1), :]))
        pad_scr[0:N_EXPERTS, :] = code_scr[N_EXPERTS * g:N_EXPERTS * (g + 1), :]
        rankt_ref[n * g:n * (g + 1), :] = pad_scr[...].T


def _route(is_sample, logits, h2):
    n = 1024 if is_sample else 256
    grp = 2 if is_sample else 8
    cap = CAPACITY_FACTOR * n // N_EXPERTS
    n_req = logits.shape[0] // n
    slots = N_EXPERTS * cap
    tri = jnp.asarray(np.triu(np.ones((n, n), np.float32), k=1), dtype=_BF16)
    return pl.pallas_call(
        functools.partial(_route_kernel, n, cap, grp),
        grid=(n_req // grp,),
        in_specs=[pl.BlockSpec((grp * n, LANES), lambda r: (r, 0)),
                  pl.BlockSpec((grp * n, D_MODEL), lambda r: (r, 0)),
                  _const_spec((n, n))],
        out_specs=[pl.BlockSpec((grp, slots, D_MODEL), lambda r: (r, 0, 0)),
                   pl.BlockSpec((grp, slots, 1), lambda r: (r, 0, 0)),
                   pl.BlockSpec((grp * n, LANES), lambda r: (r, 0))],
        out_shape=[jax.ShapeDtypeStruct((n_req, slots, D_MODEL), _BF16),
                   jax.ShapeDtypeStruct((n_req, slots, 1), _F32),
                   jax.ShapeDtypeStruct((n_req * n, LANES), _F32)],
        scratch_shapes=[pltpu.VMEM((N_EXPERTS * grp, n), _F32), pltpu.VMEM((N_EXPERTS * grp, n), _F32),
                        pltpu.VMEM((LANES, n), _F32), pltpu.VMEM((grp * slots, n), _BF16)],
        compiler_params=_params(1),
        name="route_sample" if is_sample else "route_prompt",
    )(logits, h2, tri)


def _moe_kernel(xp_ref, vp_ref, xs_ref, vs_ref, wg_ref, wu_ref, wd_ref, op_ref, os_ref):
    wg = _bf(wg_ref[...])
    wu = _bf(wu_ref[...])
    wd = _bf(wd_ref[...])
    for x_ref, v_ref, o_ref in ((xp_ref, vp_ref, op_ref), (xs_ref, vs_ref, os_ref)):
        n_req, cap, _ = x_ref.shape
        xg = x_ref[...].reshape(n_req * cap, D_MODEL)
        a = _dot(xg, wg)
        u = _dot(xg, wu)
        act = _bf(a * _sigmoid(a) * u)
        scaled = _dot(act, wd) * v_ref[...].reshape(n_req * cap, 1)
        o_ref[...] = _bf(scaled).reshape(n_req, cap, D_MODEL)


def _moe(l, xg_p, vals_p, xg_s, vals_s, w_gate, w_up, w_down):
    def tok_specs(xg):
        n_req, slots, _ = xg.shape
        cap = slots // N_EXPERTS
        return (pl.BlockSpec((n_req, cap, D_MODEL), lambda e: (0, e, 0)),
                pl.BlockSpec((n_req, cap, 1), lambda e: (0, e, 0)))
    xp_spec, vp_spec = tok_specs(xg_p)
    xs_spec, vs_spec = tok_specs(xg_s)
    w_spec = pl.BlockSpec((None, None, D_MODEL, EXPERT_HIDDEN), lambda e: (l, e, 0, 0))
    return pl.pallas_call(
        _moe_kernel,
        grid=(N_EXPERTS,),
        in_specs=[xp_spec, vp_spec, xs_spec, vs_spec, w_spec, w_spec, w_spec],
        out_specs=[xp_spec, xs_spec],
        out_shape=[jax.ShapeDtypeStruct(xg_p.shape, _BF16), jax.ShapeDtypeStruct(xg_s.shape, _BF16)],
        compiler_params=_params(1),
        name="moe_experts",
    )(xg_p, vals_p, xg_s, vals_s, w_gate, w_up, w_down)


def _combine_kernel(n, cap, reqs, o_ref, rankt_ref, x1_ref, mod_ref, x2_ref, pt_scr):
    per = LANES // cap
    lane = _lane_iota(n)
    slot = (lane % cap).astype(_F32)
    g2 = mod_ref[:, 5 * D_MODEL:6 * D_MODEL]
    for r in range(reqs):
        rank_t = rankt_ref[n * r:n * (r + 1), :]
        for grp in range(N_EXPERTS // per):
            col = jnp.broadcast_to(rank_t[:, per * grp:per * grp + 1], (n, LANES))
            for k in range(1, per):
                nxt = jnp.broadcast_to(rank_t[:, per * grp + k:per * grp + k + 1], (n, LANES))
                col = jnp.where(lane >= cap * k, nxt, col)
            pt_scr[n * r:n * (r + 1), LANES * grp:LANES * (grp + 1)] = jnp.where(col == slot, 1.0, 0.0).astype(_BF16)
        for c in range(n // ROW_BLOCK):
            rows = slice(n * r + ROW_BLOCK * c, n * r + ROW_BLOCK * (c + 1))
            y = _dot(pt_scr[rows, :], o_ref[r])
            x2_ref[rows, :] = x1_ref[rows, :] + g2 * y


def _combine(is_sample, l, o, rank_t, x1, mod4):
    n = 1024 if is_sample else 256
    cap = CAPACITY_FACTOR * n // N_EXPERTS
    n_req, slots, _ = o.shape
    reqs = 1 if is_sample else 4
    mod_row = (lambda r: (l, r, 0, 0)) if is_sample else (lambda r: (l, CTX_MOD_ROW, 0, 0))
    return pl.pallas_call(
        functools.partial(_combine_kernel, n, cap, reqs),
        grid=(n_req // reqs,),
        in_specs=[pl.BlockSpec((reqs, slots, D_MODEL), lambda r: (r, 0, 0)),
                  pl.BlockSpec((reqs * n, LANES), lambda r: (r, 0)),
                  pl.BlockSpec((reqs * n, D_MODEL), lambda r: (r, 0)),
                  pl.BlockSpec((None, None, 1, 6 * D_MODEL), mod_row)],
        out_specs=pl.BlockSpec((reqs * n, D_MODEL), lambda r: (r, 0)),
        out_shape=jax.ShapeDtypeStruct(x1.shape, _F32),
        scratch_shapes=[pltpu.VMEM((reqs * n, slots), _BF16)],
        compiler_params=_params(1),
        name="combine_sample" if is_sample else "combine_prompt",
    )(o, rank_t, x1, mod4)


def _rope_tables(n):
    pos = np.arange(n)
    row = (pos // GRID_W).astype(np.float32)
    col = (pos % GRID_W).astype(np.float32)

    def axis_tab(half, p):
        freqs = (ROPE_BASE ** (-np.arange(half, dtype=np.float32) / half)).astype(np.float32)
        ang = p[:, None] * freqs[None, :]
        c, s = np.cos(ang), np.sin(ang)
        return np.concatenate([c, c], axis=1), np.concatenate([-s, s], axis=1)

    cr, sr = axis_tab(16, row)
    cc, sc = axis_tab(16, col)
    cos_swa = np.concatenate([cr, cc, cr, cc], axis=1)
    sin_swa = np.concatenate([sr, sc, sr, sc], axis=1)
    cr, sr = axis_tab(8, row)
    cc, sc = axis_tab(8, col)
    ones = np.ones((n, 64), np.float32)
    zeros = np.zeros((n, 64), np.float32)
    cos_mla = np.concatenate([ones, cr, cc, ones[:, :32]], axis=1)
    sin_mla = np.concatenate([zeros, sr, sc, zeros[:, :32]], axis=1)
    return tuple(jnp.asarray(t, dtype=_F32) for t in (cos_swa, sin_swa, cos_mla, sin_mla))


def kernel(x_prompt, x_sample, cache_swa_k, cache_swa_v, cache_mla_ckv, cache_mla_krope, c, c_ctx, w_mod, b_mod, norm1_g, norm2_g, w_in, pool_w, pool_scale, swa_q_g, swa_k_g, sink, mla_q_norm_g, w_q_b, mla_kv_norm_g, w_kv_b, mla_q_g, mla_k_g, w_br_a, w_br_b, w_br_c, w_out, w_router, w_gate, w_up, w_down):
    n_p, seq, _ = x_prompt.shape
    n_s, dec_seq, _ = x_sample.shape
    xp = x_prompt.reshape(n_p * seq, D_MODEL)
    xs = x_sample.reshape(n_s * dec_seq, D_MODEL)

    cond = jnp.concatenate([c, c_ctx[None, :], jnp.zeros((8 - n_s - 1, D_MODEL), _F32)], axis=0)
    mod4 = _modulation(cond, w_mod, b_mod).reshape(DEPTH, 8, 1, 6 * D_MODEL)

    tabs = _rope_tables(dec_seq)
    ck = cache_swa_k.reshape(n_s, DEPTH, PAST_LEN, 128)
    cv = cache_swa_v.reshape(n_s, DEPTH, PAST_LEN, 128)
    ckr = jnp.pad(cache_mla_krope, ((0, 0), (0, 0), (0, 0), (64, 32)))
    caches = (ck, cv, cache_mla_ckv, ckr)

    vec = lambda v: v.reshape(DEPTH, 1, -1)
    pad_head = lambda g: jnp.pad(g, ((0, 0), (0, LANES - MLA_QK_DIM)))
    wqb = jnp.pad(w_q_b.reshape(DEPTH, MLA_Q_RANK, MLA_HEADS, MLA_QK_DIM),
                  ((0, 0), (0, 0), (0, 0), (0, LANES - MLA_QK_DIM)))
    vecs1 = (vec(norm1_g), vec(jnp.tile(swa_q_g, (1, 2))), vec(jnp.tile(swa_k_g, (1, 2))),
             vec(mla_q_norm_g), vec(mla_kv_norm_g), vec(pad_head(mla_q_g)), vec(pad_head(mla_k_g)),
             vec(pool_scale))
    mats1 = (_bf(w_in[:, :, :ZA_WIDTH]), _bf(wqb.reshape(DEPTH, MLA_Q_RANK, MLA_HEADS * LANES)),
             _bf(w_kv_b), _bf(pool_w))
    vecs2 = (vec(norm1_g), vec(norm2_g))
    wg = _bf(w_in[:, :, GATE_OFF:])
    mats2 = (_bf(w_br_a), _bf(w_br_b), _bf(w_br_c), _bf(w_out),
             _bf(jnp.pad(w_router, ((0, 0), (0, 0), (0, LANES - N_EXPERTS)))))

    new_k, new_v, new_ckv, new_kr = [], [], [], []
    for l in range(DEPTH):
        out_p = _tm1(False, l, xp, mod4, vecs1, mats1, None)
        out_s = _tm1(True, l, xs, mod4, vecs1, mats1, (tabs, caches))
        new_k.append(out_p[7].reshape(n_p, seq, 2, 64))
        new_v.append(out_p[8].reshape(n_p, seq, 2, 64))
        new_ckv.append(out_p[9].reshape(n_p, seq, MLA_KV_RANK))
        new_kr.append(out_p[10].reshape(n_p, seq, MLA_ROPE_DIM))

        x1p, h2p, lgp = _tm2(False, l, xp, mod4, sink, vecs2, wg, out_p[:7], mats2)
        x1s, h2s, lgs = _tm2(True, l, xs, mod4, sink, vecs2, wg, out_s[:7], mats2)

        xg_p, vals_p, rt_p = _route(False, lgp, h2p)
        xg_s, vals_s, rt_s = _route(True, lgs, h2s)
        o_p, o_s = _moe(l, xg_p, vals_p, xg_s, vals_s, w_gate, w_up, w_down)
        xp = _combine(False, l, o_p, rt_p, x1p, mod4)
        xs = _combine(True, l, o_s, rt_s, x1s, mod4)

    return (xp.reshape(n_p, seq, D_MODEL), xs.reshape(n_s, dec_seq, D_MODEL),
            jnp.stack(new_k, axis=1), jnp.stack(new_v, axis=1),
            jnp.stack(new_ckv, axis=1), jnp.stack(new_kr, axis=1))
```

```python
import functools

import numpy as np
import jax
import jax.numpy as jnp
from jax import lax
from jax.experimental import pallas as pl
from jax.experimental.pallas import tpu as pltpu

D_MODEL = 1024
DEPTH = 2
GRID_W = 64
ROPE_BASE = 10000.0
EPS = 1e-6
NEG_INF = -1e30
POOL_WINDOWS = (2, 4, 8, 16)
POOL_WIDTH = 512
SWA_HEADS = 8
SWA_HEAD_DIM = 64
SWA_WINDOW = 128
MLA_HEADS = 8
MLA_Q_RANK = 384
MLA_KV_RANK = 256
MLA_NOPE_DIM = 64
MLA_ROPE_DIM = 32
MLA_QK_DIM = MLA_NOPE_DIM + MLA_ROPE_DIM
N_EXPERTS = 16
EXPERT_HIDDEN = 1024
CAPACITY_FACTOR = 2
PAST_LEN = 256

LANES = 128
ROW_BLOCK = 256
QUERY_BLOCK = 128
POOL_PAD = 16
ZA_WIDTH = 2048
GATE_OFF = 1952
VMEM_LIMIT = 56 * 1024 * 1024
CTX_MOD_ROW = 4
LOG2E = 1.4426950408889634

_F32 = jnp.float32
_BF16 = jnp.bfloat16


def _bf(x):
    return x.astype(_BF16)


def _dot(a, b):
    return jnp.dot(a, b, preferred_element_type=_F32)


def _dot_nt(a, b):
    return lax.dot_general(a, b, (((1,), (1,)), ((), ())), preferred_element_type=_F32)


def _sigmoid(x):
    return 1.0 / (1.0 + jnp.exp(-x))


def _rms(x, g):
    return x * lax.rsqrt(jnp.mean(x * x, axis=-1, keepdims=True) + EPS) * g


def _mod_norm(x, gain_mod, shift):
    return _bf(x * lax.rsqrt(jnp.mean(x * x, axis=-1, keepdims=True) + EPS) * gain_mod + shift)


def _lane_iota(rows):
    return lax.broadcasted_iota(jnp.int32, (rows, LANES), 1)


def _swap_lanes(x, seg):
    lane = _lane_iota(x.shape[0])
    up = pltpu.roll(x, LANES - seg, 1)
    dn = pltpu.roll(x, seg, 1)
    return jnp.where((lane & seg) == 0, up, dn)


def _rope(x, cos, sin_signed, seg):
    return x * cos + _swap_lanes(x, seg) * sin_signed


def _seg_ones(seg, rows_used=LANES):
    r = lax.broadcasted_iota(jnp.int32, (LANES, LANES), 0)
    c = lax.broadcasted_iota(jnp.int32, (LANES, LANES), 1)
    return jnp.where(((r // seg) == (c // seg)) & (r < rows_used), 1.0, 0.0).astype(_BF16)


def _seg_sumsq(x, ones):
    return _dot(_bf(x * x), ones)


def _half_norm(x, gain, ones64=None):
    if ones64 is not None:
        return x * lax.rsqrt(_seg_sumsq(x, ones64) / SWA_HEAD_DIM + EPS) * gain
    lo = _lane_iota(x.shape[0]) < 64
    sq = x * x
    s_lo = jnp.sum(jnp.where(lo, sq, 0.0), axis=-1, keepdims=True)
    s_hi = jnp.sum(jnp.where(lo, 0.0, sq), axis=-1, keepdims=True)
    r = jnp.where(lo, lax.rsqrt(s_lo / SWA_HEAD_DIM + EPS), lax.rsqrt(s_hi / SWA_HEAD_DIM + EPS))
    return x * r * gain


def _dup_halves(x):
    lo = _lane_iota(x.shape[0]) < 64
    sw = pltpu.roll(x, 64, 1)
    return jnp.where(lo, x, sw), jnp.where(lo, sw, x)


def _const_spec(shape):
    return pl.BlockSpec(shape, lambda *_: (0,) * len(shape))


def _layer_spec(arr, l):
    return pl.BlockSpec((None,) + arr.shape[1:], lambda *_: (l,) + (0,) * (arr.ndim - 1))


def _params(n_axes):
    return pltpu.CompilerParams(dimension_semantics=("arbitrary",) * n_axes, vmem_limit_bytes=VMEM_LIMIT)


def _mod_kernel(cond_ref, w_ref, b_ref, o_ref):
    cnd = cond_ref[...]
    act = _bf(cnd * _sigmoid(cnd))
    o_ref[...] = _dot(act, _bf(w_ref[...])) + b_ref[...]


def _modulation(cond, w_mod, b_mod):
    n_out = w_mod.shape[-1]
    tn = 1536
    return pl.pallas_call(
        _mod_kernel,
        grid=(DEPTH, n_out // tn),
        in_specs=[
            pl.BlockSpec((8, D_MODEL), lambda l, j: (0, 0)),
            pl.BlockSpec((None, D_MODEL, tn), lambda l, j: (l, 0, j)),
            pl.BlockSpec((None, 1, tn), lambda l, j: (l, 0, j)),
        ],
        out_specs=pl.BlockSpec((None, 8, tn), lambda l, j: (l, 0, j)),
        out_shape=jax.ShapeDtypeStruct((DEPTH, 8, n_out), _F32),
        compiler_params=_params(2),
        name="modulation",
    )(cond, w_mod, b_mod.reshape(DEPTH, 1, n_out))


def _mla_kv_store(ckv, krr, wkvb_ref, mkg, rope_tabs, km_ref, vm_ref, rows, mxu_sums):
    n_rows = ckv.shape[0]
    lo = _lane_iota(n_rows) < 64
    kv = _dot(_bf(ckv), wkvb_ref[...])
    krg = krr * mkg
    if rope_tabs is not None:
        krg = _rope(krg, rope_tabs[0], rope_tabs[1], 8)
    if mxu_sums:
        ss_kr = _seg_sumsq(krr, _seg_ones(LANES))
        ones_lo = _seg_ones(LANES, 64)
        mkg_nope = jnp.where(_lane_iota(1) < 64, mkg, 0.0)
    else:
        ss_kr = jnp.sum(krr * krr, axis=-1, keepdims=True)
    for h in range(MLA_HEADS):
        slab = kv[:, LANES * h:LANES * (h + 1)]
        if mxu_sums:
            r = lax.rsqrt((_seg_sumsq(slab, ones_lo) + ss_kr) / MLA_QK_DIM + EPS)
            km_ref[rows, LANES * h:LANES * (h + 1)] = _bf((slab * mkg_nope + krg) * r)
        else:
            nope = jnp.where(lo, slab, 0.0)
            ss = jnp.sum(nope * nope, axis=-1, keepdims=True) + ss_kr
            r = lax.rsqrt(ss / MLA_QK_DIM + EPS)
            km_ref[rows, LANES * h:LANES * (h + 1)] = _bf((nope * mkg + krg) * r)
    for j in range(MLA_HEADS // 2):
        a = kv[:, 2 * LANES * j:2 * LANES * j + LANES]
        b = kv[:, 2 * LANES * j + LANES:2 * LANES * (j + 1)]
        vm_ref[rows, LANES * j:LANES * (j + 1)] = _bf(jnp.where(lo, pltpu.roll(a, 64, 1), b))


def _tm1_kernel(is_sample, n, reqs, *refs):
    if is_sample:
        (x_ref, mod_ref, n1g_ref, qg_ref, kg_ref, qng_ref, kvng_ref, mqg_ref, mkg_ref, psc_ref,
         wa_ref, wqb_ref, wkvb_ref, pw_ref,
         cs_ref, ss_ref, cm_ref, sm_ref, ck_ref, cv_ref, cckv_ref, ckr_ref,
         qs_ref, k_ref, v_ref, qm_ref, km_ref, vm_ref, ya_ref, pad_scr) = refs
    else:
        (x_ref, mod_ref, n1g_ref, qg_ref, kg_ref, qng_ref, kvng_ref, mqg_ref, mkg_ref, psc_ref,
         wa_ref, wqb_ref, wkvb_ref, pw_ref,
         qs_ref, k_ref, v_ref, qm_ref, km_ref, vm_ref, ya_ref,
         nk_ref, nv_ref, nckv_ref, nkr_ref, pad_scr) = refs

    sh1 = mod_ref[:, 0:D_MODEL]
    gm1 = n1g_ref[...] * (1.0 + mod_ref[:, D_MODEL:2 * D_MODEL])
    qg = qg_ref[...] * (SWA_HEAD_DIM ** -0.5 * LOG2E)
    kg = kg_ref[...]
    mqg = mqg_ref[...] * (MLA_QK_DIM ** -0.5 * LOG2E)
    mkg = mkg_ref[...]

    mxu_sums = is_sample
    ones64 = _seg_ones(64) if mxu_sums else None
    ones128 = _seg_ones(LANES) if mxu_sums else None
    bpr = n // ROW_BLOCK
    pad_rows = n + 2 * POOL_PAD
    zeros_pad = jnp.zeros((POOL_PAD, POOL_WIDTH), _F32)
    for q in range(reqs):
        pad_scr[pad_rows * q:pad_rows * q + POOL_PAD, :] = zeros_pad
        pad_scr[pad_rows * q + POOL_PAD + n:pad_rows * (q + 1), :] = zeros_pad

    def in_proj(j):
        x = x_ref[ROW_BLOCK * j:ROW_BLOCK * (j + 1), :]
        return _dot(_mod_norm(x, gm1, sh1), wa_ref[...])

    def sub_block(j, z):
        q = j // bpr
        tok = slice(ROW_BLOCK * j, ROW_BLOCK * (j + 1))
        rows = slice(ROW_BLOCK * (j % bpr), ROW_BLOCK * (j % bpr + 1))
        pad0 = pad_rows * q + POOL_PAD + rows.start
        pad_scr[pad0:pad0 + ROW_BLOCK, :] = z[:, 0:POOL_WIDTH]

        if is_sample:
            cs = cs_ref[rows, :]
            ss = ss_ref[rows, :]
            cm = cm_ref[rows, :]
            sm = sm_ref[rows, :]

        lo = _lane_iota(ROW_BLOCK) < 64
        for c in range(4):
            qn = _half_norm(z[:, 512 + LANES * c:512 + LANES * (c + 1)], qg, ones64)
            if is_sample:
                qn = _rope(qn, cs, ss, 16)
            qs_ref[q, 2 * c, rows, :] = _bf(jnp.where(lo, qn, 0.0))
            qs_ref[q, 2 * c + 1, rows, :] = _bf(jnp.where(lo, 0.0, qn))
        kn = _half_norm(z[:, 1024:1152], kg, ones64)
        vv = z[:, 1152:1280]
        if is_sample:
            kn = _rope(kn, cs, ss, 16)
        else:
            nk_ref[tok, :] = kn
            nv_ref[tok, :] = vv
        k0, k1 = _dup_halves(kn)
        k_ref[q, rows, 0:LANES] = _bf(k0)
        k_ref[q, rows, LANES:2 * LANES] = _bf(k1)
        v0, v1 = _dup_halves(vv)
        v_ref[q, rows, 0:LANES] = _bf(v0)
        v_ref[q, rows, LANES:2 * LANES] = _bf(v1)

        qcn = _bf(_rms(z[:, 1280:1664], qng_ref[...]))
        qm = _dot(qcn, wqb_ref[...])
        for hh in range(MLA_HEADS):
            t = qm[:, LANES * hh:LANES * (hh + 1)]
            ss = _seg_sumsq(t, ones128) if mxu_sums else jnp.sum(t * t, axis=-1, keepdims=True)
            tn = t * lax.rsqrt(ss / MLA_QK_DIM + EPS) * mqg
            if is_sample:
                tn = _rope(tn, cm, sm, 8)
            qm_ref[tok, LANES * hh:LANES * (hh + 1)] = _bf(tn)

        ckv = _rms(z[:, 1664:1920], kvng_ref[...])
        krs = jnp.where(_lane_iota(ROW_BLOCK) < MLA_ROPE_DIM, z[:, 1920:2048], 0.0)
        if not is_sample:
            nckv_ref[tok, :] = ckv
            nkr_ref[tok, :] = krs[:, 0:MLA_ROPE_DIM]
        krr = pltpu.roll(krs, 64, 1)
        _mla_kv_store(ckv, krr, wkvb_ref, mkg, (cm, sm) if is_sample else None,
                      km_ref.at[q], vm_ref.at[q], rows, mxu_sums)

    n_blocks = reqs * bpr
    z_next = in_proj(0)
    for j in range(n_blocks):
        z_cur = z_next
        if j + 1 < n_blocks:
            z_next = in_proj(j + 1)
        sub_block(j, z_cur)

    if is_sample:
        ctx_rows = slice(n, n + PAST_LEN)
        c0, c1 = _dup_halves(ck_ref[...])
        k_ref[0, ctx_rows, 0:LANES] = _bf(c0)
        k_ref[0, ctx_rows, LANES:2 * LANES] = _bf(c1)
        c0, c1 = _dup_halves(cv_ref[...])
        v_ref[0, ctx_rows, 0:LANES] = _bf(c0)
        v_ref[0, ctx_rows, LANES:2 * LANES] = _bf(c1)
        _mla_kv_store(cckv_ref[...], ckr_ref[...], wkvb_ref, mkg, None, km_ref.at[0], vm_ref.at[0], ctx_rows,
                      mxu_sums)

    for j in range(n_blocks):
        q, c = j // bpr, j % bpr
        base = pad_rows * q + POOL_PAD + ROW_BLOCK * c
        t = lax.broadcasted_iota(jnp.int32, (ROW_BLOCK, 1), 0) + ROW_BLOCK * c
        for g, w in enumerate(POOL_WINDOWS):
            half = w // 2
            cols = slice(LANES * g, LANES * (g + 1))
            acc = pad_scr[base - half:base - half + ROW_BLOCK, cols]
            for jj in range(1 - half, half):
                acc = acc + pad_scr[base + jj:base + jj + ROW_BLOCK, cols]
            u = pad_scr[base:base + ROW_BLOCK, cols]
            cnt = (jnp.minimum(t + half, n) - jnp.maximum(t - half, 0)).astype(_F32)
            dlt = _bf(acc / cnt - u)
            ya = _dot(dlt, pw_ref[g]) * psc_ref[:, cols]
            ya_ref[ROW_BLOCK * j:ROW_BLOCK * (j + 1), cols] = _bf(ya)


def _tm1(is_sample, l, x2, mod4, vecs, mats, extra):
    n = 1024 if is_sample else 256
    reqs = 1 if is_sample else 4
    n_req = x2.shape[0] // n
    nk = n + PAST_LEN if is_sample else n
    mod_row = (lambda r: (l, r, 0, 0)) if is_sample else (lambda r: (l, CTX_MOD_ROW, 0, 0))
    in_specs = [pl.BlockSpec((reqs * n, D_MODEL), lambda r: (r, 0)),
                pl.BlockSpec((None, None, 1, 6 * D_MODEL), mod_row)]
    in_specs += [_layer_spec(v, l) for v in vecs]
    in_specs += [_layer_spec(m, l) for m in mats]
    if is_sample:
        tabs, caches = extra
        in_specs += [_const_spec(t.shape) for t in tabs]
        in_specs += [pl.BlockSpec((None, None, PAST_LEN, c.shape[-1]), lambda r: (r, l, 0, 0)) for c in caches]
        operands = (x2, mod4, *vecs, *mats, *tabs, *caches)
    else:
        operands = (x2, mod4, *vecs, *mats)
    tok = n_req * n
    out_shape = [jax.ShapeDtypeStruct((n_req, SWA_HEADS, n, LANES), _BF16)]
    out_specs = [pl.BlockSpec((reqs, SWA_HEADS, n, LANES), lambda r: (r, 0, 0, 0))]
    for width in (256, 256):
        out_shape.append(jax.ShapeDtypeStruct((n_req, nk, width), _BF16))
        out_specs.append(pl.BlockSpec((reqs, nk, width), lambda r: (r, 0, 0)))
    out_shape.append(jax.ShapeDtypeStruct((tok, 1024), _BF16))
    out_specs.append(pl.BlockSpec((reqs * n, 1024), lambda r: (r, 0)))
    for width in (1024, 512):
        out_shape.append(jax.ShapeDtypeStruct((n_req, nk, width), _BF16))
        out_specs.append(pl.BlockSpec((reqs, nk, width), lambda r: (r, 0, 0)))
    out_shape.append(jax.ShapeDtypeStruct((tok, 512), _BF16))
    out_specs.append(pl.BlockSpec((reqs * n, 512), lambda r: (r, 0)))
    if not is_sample:
        for width in (128, 128, MLA_KV_RANK, MLA_ROPE_DIM):
            out_shape.append(jax.ShapeDtypeStruct((tok, width), _F32))
            out_specs.append(pl.BlockSpec((reqs * n, width), lambda r: (r, 0)))
    return pl.pallas_call(
        functools.partial(_tm1_kernel, is_sample, n, reqs),
        grid=(n_req // reqs,),
        in_specs=in_specs,
        out_specs=out_specs,
        out_shape=out_shape,
        scratch_shapes=[pltpu.VMEM((reqs * (n + 2 * POOL_PAD), POOL_WIDTH), _F32)],
        compiler_params=_params(1),
        name="tm1_sample" if is_sample else "tm1_prompt",
    )(*operands)


def _softmax_pv(parts, sink):
    m = None
    for s, _ in parts:
        mi = jnp.max(s, axis=-1, keepdims=True)
        m = mi if m is None else jnp.maximum(m, mi)
    if sink is not None:
        m = jnp.maximum(m, sink)
    den = None
    acc = None
    for s, v in parts:
        p = jnp.exp2(s - m)
        d = jnp.sum(p, axis=-1, keepdims=True)
        den = d if den is None else den + d
        o = _dot(_bf(p), v)
        acc = o if acc is None else acc + o
    if sink is not None:
        den = den + jnp.exp2(sink - m)
    return acc / den


def _tm2_kernel(is_sample, l, nsub, *refs):
    (sink_ref, x_ref, mod_ref, n1g_ref, n2g_ref, wg_ref, qs_ref, k_ref, v_ref, qm_ref, km_ref, vm_ref,
     ya_ref, wba_ref, wbb_ref, wbc_ref, wo_ref, wr_ref,
     x1_ref, h2_ref, lg_ref, yb_scr, yc_scr) = refs
    n_own = 1024
    q_rows = QUERY_BLOCK if is_sample else ROW_BLOCK
    grp_heads = SWA_HEADS // 2
    lo_q = _lane_iota(q_rows) < 64
    lo = _lane_iota(ROW_BLOCK) < 64
    head_of_row = lax.broadcasted_iota(jnp.int32, (grp_heads * q_rows, 1), 0) // q_rows
    sh1 = mod_ref[:, 0:D_MODEL]
    gm1 = n1g_ref[...] * (1.0 + mod_ref[:, D_MODEL:2 * D_MODEL])
    g1 = mod_ref[:, 2 * D_MODEL:3 * D_MODEL]
    sh2 = mod_ref[:, 3 * D_MODEL:4 * D_MODEL]
    gm2 = n2g_ref[...] * (1.0 + mod_ref[:, 4 * D_MODEL:5 * D_MODEL])

    for u in range(nsub):
        req = 0 if is_sample else u
        q_off = ROW_BLOCK * u if is_sample else 0
        tok = slice(ROW_BLOCK * u, ROW_BLOCK * (u + 1))
        k_r, v_r, km_r, vm_r = k_ref.at[req], v_ref.at[req], km_ref.at[req], vm_ref.at[req]

        def swa_chain_inputs(sb, kvh):
            rows = slice(q_off + q_rows * sb, q_off + q_rows * (sb + 1))
            kcols = slice(LANES * kvh, LANES * (kvh + 1))
            q4 = qs_ref[req, grp_heads * kvh:grp_heads * (kvh + 1), rows, :].reshape(grp_heads * q_rows, LANES)
            sink = jnp.full((grp_heads * q_rows, 1), sink_ref[l, grp_heads * kvh] * LOG2E, _F32)
            for hh in range(1, grp_heads):
                sink = jnp.where(head_of_row >= hh, sink_ref[l, grp_heads * kvh + hh] * LOG2E, sink)
            if not is_sample:
                return [(_dot_nt(q4, k_r[:, kcols]), v_r[:, kcols])], sink
            row_blk = (pl.program_id(0) * nsub + u) % (n_own // ROW_BLOCK)
            blk = row_blk * 2 + sb
            start = jnp.clip((blk - 1) * QUERY_BLOCK, 0, n_own - 3 * QUERY_BLOCK)
            start = pl.multiple_of(start, QUERY_BLOCK)
            shape = (grp_heads * QUERY_BLOCK, 3 * QUERY_BLOCK)
            qpos = blk * QUERY_BLOCK + (lax.broadcasted_iota(jnp.int32, shape, 0) & (QUERY_BLOCK - 1))
            kpos = start + lax.broadcasted_iota(jnp.int32, shape, 1)
            band = jnp.abs(kpos - qpos) <= SWA_WINDOW
            s_loc = _dot_nt(q4, k_r[pl.ds(start, 3 * QUERY_BLOCK), kcols])
            s_loc = jnp.where(band, s_loc, NEG_INF)
            s_ctx = _dot_nt(q4, k_r[n_own:n_own + PAST_LEN, kcols])
            return [(s_loc, v_r[pl.ds(start, 3 * QUERY_BLOCK), kcols]),
                    (s_ctx, v_r[n_own:n_own + PAST_LEN, kcols])], sink

        chains = [(sb, kvh) for sb in range(ROW_BLOCK // q_rows) for kvh in range(2)]
        nxt = swa_chain_inputs(*chains[0])
        for ci, (sb, kvh) in enumerate(chains):
            parts, sink = nxt
            if ci + 1 < len(chains):
                nxt = swa_chain_inputs(*chains[ci + 1])
            rows = slice(ROW_BLOCK * u + q_rows * sb, ROW_BLOCK * u + q_rows * (sb + 1))
            o4 = _softmax_pv(parts, sink)
            for pair in range(grp_heads // 2):
                even = o4[q_rows * 2 * pair:q_rows * (2 * pair + 1)]
                odd = o4[q_rows * (2 * pair + 1):q_rows * (2 * pair + 2)]
                col = LANES * (2 * kvh + pair)
                yb_scr[rows, col:col + LANES] = _bf(jnp.where(lo_q, even, odd))

        def mla_scores(hh):
            return _dot_nt(qm_ref[tok, LANES * hh:LANES * (hh + 1)], km_r[:, LANES * hh:LANES * (hh + 1)])

        lookahead = is_sample
        s_next = mla_scores(0)
        outs = []
        for hh in range(MLA_HEADS):
            s_cur = s_next if lookahead or hh == 0 else mla_scores(hh)
            if lookahead and hh + 1 < MLA_HEADS:
                s_next = mla_scores(hh + 1)
            j = hh // 2
            outs.append(_softmax_pv([(s_cur, vm_r[:, LANES * j:LANES * (j + 1)])], None))
            if hh % 2 == 1:
                yc_scr[tok, LANES * j:LANES * (j + 1)] = _bf(jnp.where(lo, outs[0], outs[1]))
                outs = []

        x = x_ref[tok, :]
        h = _mod_norm(x, gm1, sh1)
        merged = _sigmoid(_dot(h, wg_ref[:, 0:D_MODEL])) * _dot(ya_ref[tok, :], wba_ref[...])
        merged = merged + _sigmoid(_dot(h, wg_ref[:, D_MODEL:2 * D_MODEL])) * _dot(yb_scr[tok, :], wbb_ref[...])
        merged = merged + _sigmoid(_dot(h, wg_ref[:, 2 * D_MODEL:3 * D_MODEL])) * _dot(yc_scr[tok, :], wbc_ref[...])
        x1 = x + g1 * _dot(_bf(merged), wo_ref[...])
        x1_ref[tok, :] = x1
        h2 = _mod_norm(x1, gm2, sh2)
        h2_ref[tok, :] = h2
        lg_ref[tok, :] = _dot(h2, wr_ref[...])


def _tm2(is_sample, l, x2, mod4, sink, vecs, wg, tm1_out, mats):
    qs, kd, vd, qm, km, vm, ya = tm1_out
    n = 1024 if is_sample else 256
    nsub = 2
    rows = nsub * ROW_BLOCK
    nk = kd.shape[1]
    tok = x2.shape[0]
    if is_sample:
        steps_per_req = n // rows
        mod_row = lambda i: (l, i // steps_per_req, 0, 0)
        qs_blk = pl.BlockSpec((1, SWA_HEADS, rows, LANES), lambda i: (i // steps_per_req, 0, i % steps_per_req, 0))
        req_blk = lambda width: pl.BlockSpec((1, nk, width), lambda i: (i // steps_per_req, 0, 0))
    else:
        mod_row = lambda i: (l, CTX_MOD_ROW, 0, 0)
        qs_blk = pl.BlockSpec((nsub, SWA_HEADS, ROW_BLOCK, LANES), lambda i: (i, 0, 0, 0))
        req_blk = lambda width: pl.BlockSpec((nsub, nk, width), lambda i: (i, 0, 0))
    row_blk = lambda width: pl.BlockSpec((rows, width), lambda i: (i, 0))
    in_specs = [pl.BlockSpec(memory_space=pltpu.SMEM),
                row_blk(D_MODEL),
                pl.BlockSpec((None, None, 1, 6 * D_MODEL), mod_row)]
    in_specs += [_layer_spec(v, l) for v in vecs]
    in_specs += [_layer_spec(wg, l), qs_blk, req_blk(256), req_blk(256),
                 row_blk(1024), req_blk(1024), req_blk(512), row_blk(512)]
    in_specs += [_layer_spec(m, l) for m in mats]
    return pl.pallas_call(
        functools.partial(_tm2_kernel, is_sample, l, nsub),
        grid=(tok // rows,),
        in_specs=in_specs,
        out_specs=[row_blk(D_MODEL), row_blk(D_MODEL), row_blk(LANES)],
        out_shape=[jax.ShapeDtypeStruct((tok, D_MODEL), _F32),
                   jax.ShapeDtypeStruct((tok, D_MODEL), _BF16),
                   jax.ShapeDtypeStruct((tok, LANES), _F32)],
        scratch_shapes=[pltpu.VMEM((rows, 512), _BF16), pltpu.VMEM((rows, 512), _BF16)],
        compiler_params=_params(1),
        name="tm2_sample" if is_sample else "tm2_prompt",
    )(sink, x2, mod4, *vecs, wg, qs, kd, vd, qm, km, vm, ya, *mats)


def _sort_lanes_desc(slabs):
    rows = slabs[0].shape[0]
    n = LANES * len(slabs)
    lane = _lane_iota(rows)
    k = 2
    while k <= n:
        j = k // 2
        while j >= 1:
            out = []
            for c, x in enumerate(slabs):
                if j < LANES:
                    partner = _swap_lanes(x, j)
                    if k < LANES:
                        take_max = (((lane & j) * (k // j)) ^ (lane & k)) == 0
                    elif (LANES * c) & k == 0:
                        take_max = (lane & j) == 0
                    else:
                        take_max = (lane & j) != 0
                    out.append(jnp.where(take_max, jnp.maximum(x, partner), jnp.minimum(x, partner)))
                else:
                    partner = slabs[c ^ (j // LANES)]
                    lower = (LANES * c) & j == 0
                    desc = (LANES * c) & k == 0
                    out.append(jnp.maximum(x, partner) if lower == desc else jnp.minimum(x, partner))
            slabs = out
            j //= 2
        k *= 2
    return slabs


def _route_kernel(n, cap, grp, lg_ref, h2_ref, tri_ref, xg_ref, vals_ref, rankt_ref,
                  a_scr, code_scr, pad_scr, p_scr):
    rg = N_EXPERTS * grp
    slots = N_EXPERTS * cap
    lane = _lane_iota(grp * n)
    logits = jnp.where(lane < N_EXPERTS, lg_ref[...], -jnp.inf)
    m = jnp.max(logits, axis=-1, keepdims=True)
    e = jnp.exp(logits - m)
    aff = e / jnp.sum(e, axis=-1, keepdims=True)
    for g in range(grp):
        a_scr[N_EXPERTS * g:N_EXPERTS * (g + 1), :] = aff[n * g:n * (g + 1), :].T[0:N_EXPERTS, :]
    a = a_scr[...]

    thr = _sort_lanes_desc([a[:, LANES * c:LANES * (c + 1)] for c in range(n // LANES)])[0][:, cap - 1:cap]
    capf = float(cap)

    gt = a > thr
    eqf = jnp.where(a == thr, 1.0, 0.0)
    need = capf - jnp.sum(jnp.where(gt, 1.0, 0.0), axis=1, keepdims=True)
    eq_before = _dot(_bf(eqf), tri_ref[...])
    sel = jnp.where(gt, 1.0, jnp.where(eq_before < need, eqf, 0.0))
    pos = _dot(_bf(sel), tri_ref[...])
    code_scr[...] = jnp.where(sel > 0.5, pos, float(n))

    pad_scr[N_EXPERTS:LANES, :] = jnp.full((LANES - N_EXPERTS, n), float(n), _F32)
    slot = lax.broadcasted_iota(jnp.int32, (cap, n), 0).astype(_F32)
    chunk = min(slots, 512)
    for g in range(grp):
        for ex in range(N_EXPERTS):
            r = N_EXPERTS * g + ex
            hit = code_scr[r:r + 1, :] == slot
            p_scr[slots * g + cap * ex:slots * g + cap * (ex + 1), :] = jnp.where(hit, 1.0, 0.0).astype(_BF16)
            vals_ref[g, cap * ex:cap * (ex + 1), :] = jnp.sum(
                jnp.where(hit, a_scr[r:r + 1, :], 0.0), axis=1, keepdims=True)
        for c in range(slots // chunk):
            rows = slice(chunk * c, chunk * (c + 1))
            xg_ref[g, rows, :] = _bf(_dot(p_scr[slots * g + chunk * c:slots * g + chunk * (c + 1), :],
                                          h2_ref[n * g:n * (g + 1), :]))
        pad_scr[0:N_EXPERTS, :] = code_scr[N_EXPERTS * g:N_EXPERTS * (g + 1), :]
        rankt_ref[n * g:n * (g + 1), :] = pad_scr[...].T


def _route(is_sample, logits, h2):
    n = 1024 if is_sample else 256
    grp = 2 if is_sample else 8
    cap = CAPACITY_FACTOR * n // N_EXPERTS
    n_req = logits.shape[0] // n
    slots = N_EXPERTS * cap
    tri = jnp.asarray(np.triu(np.ones((n, n), np.float32), k=1), dtype=_BF16)
    return pl.pallas_call(
        functools.partial(_route_kernel, n, cap, grp),
        grid=(n_req // grp,),
        in_specs=[pl.BlockSpec((grp * n, LANES), lambda r: (r, 0)),
                  pl.BlockSpec((grp * n, D_MODEL), lambda r: (r, 0)),
                  _const_spec((n, n))],
        out_specs=[pl.BlockSpec((grp, slots, D_MODEL), lambda r: (r, 0, 0)),
                   pl.BlockSpec((grp, slots, 1), lambda r: (r, 0, 0)),
                   pl.BlockSpec((grp * n, LANES), lambda r: (r, 0))],
        out_shape=[jax.ShapeDtypeStruct((n_req, slots, D_MODEL), _BF16),
                   jax.ShapeDtypeStruct((n_req, slots, 1), _F32),
                   jax.ShapeDtypeStruct((n_req * n, LANES), _F32)],
        scratch_shapes=[pltpu.VMEM((N_EXPERTS * grp, n), _F32), pltpu.VMEM((N_EXPERTS * grp, n), _F32),
                        pltpu.VMEM((LANES, n), _F32), pltpu.VMEM((grp * slots, n), _BF16)],
        compiler_params=_params(1),
        name="route_sample" if is_sample else "route_prompt",
    )(logits, h2, tri)


def _moe_kernel(xp_ref, vp_ref, xs_ref, vs_ref, wg_ref, wu_ref, wd_ref, op_ref, os_ref):
    wg = _bf(wg_ref[...])
    wu = _bf(wu_ref[...])
    wd = _bf(wd_ref[...])
    for x_ref, v_ref, o_ref in ((xp_ref, vp_ref, op_ref), (xs_ref, vs_ref, os_ref)):
        n_req, cap, _ = x_ref.shape
        xg = x_ref[...].reshape(n_req * cap, D_MODEL)
        a = _dot(xg, wg)
        u = _dot(xg, wu)
        act = _bf(a * _sigmoid(a) * u)
        scaled = _dot(act, wd) * v_ref[...].reshape(n_req * cap, 1)
        o_ref[...] = _bf(scaled).reshape(n_req, cap, D_MODEL)


def _moe(l, xg_p, vals_p, xg_s, vals_s, w_gate, w_up, w_down):
    def tok_specs(xg):
        n_req, slots, _ = xg.shape
        cap = slots // N_EXPERTS
        return (pl.BlockSpec((n_req, cap, D_MODEL), lambda e: (0, e, 0)),
                pl.BlockSpec((n_req, cap, 1), lambda e: (0, e, 0)))
    xp_spec, vp_spec = tok_specs(xg_p)
    xs_spec, vs_spec = tok_specs(xg_s)
    w_spec = pl.BlockSpec((None, None, D_MODEL, EXPERT_HIDDEN), lambda e: (l, e, 0, 0))
    return pl.pallas_call(
        _moe_kernel,
        grid=(N_EXPERTS,),
        in_specs=[xp_spec, vp_spec, xs_spec, vs_spec, w_spec, w_spec, w_spec],
        out_specs=[xp_spec, xs_spec],
        out_shape=[jax.ShapeDtypeStruct(xg_p.shape, _BF16), jax.ShapeDtypeStruct(xg_s.shape, _BF16)],
        compiler_params=_params(1),
        name="moe_experts",
    )(xg_p, vals_p, xg_s, vals_s, w_gate, w_up, w_down)


def _combine_kernel(n, cap, reqs, o_ref, rankt_ref, x1_ref, mod_ref, x2_ref, pt_scr):
    per = LANES // cap
    lane = _lane_iota(n)
    slot = (lane % cap).astype(_F32)
    g2 = mod_ref[:, 5 * D_MODEL:6 * D_MODEL]
    for r in range(reqs):
        rank_t = rankt_ref[n * r:n * (r + 1), :]
        for grp in range(N_EXPERTS // per):
            col = jnp.broadcast_to(rank_t[:, per * grp:per * grp + 1], (n, LANES))
            for k in range(1, per):
                nxt = jnp.broadcast_to(rank_t[:, per * grp + k:per * grp + k + 1], (n, LANES))
                col = jnp.where(lane >= cap * k, nxt, col)
            pt_scr[n * r:n * (r + 1), LANES * grp:LANES * (grp + 1)] = jnp.where(col == slot, 1.0, 0.0).astype(_BF16)
        for c in range(n // ROW_BLOCK):
            rows = slice(n * r + ROW_BLOCK * c, n * r + ROW_BLOCK * (c + 1))
            y = _dot(pt_scr[rows, :], o_ref[r])
            x2_ref[rows, :] = x1_ref[rows, :] + g2 * y


def _combine(is_sample, l, o, rank_t, x1, mod4):
    n = 1024 if is_sample else 256
    cap = CAPACITY_FACTOR * n // N_EXPERTS
    n_req, slots, _ = o.shape
    reqs = 1 if is_sample else 4
    mod_row = (lambda r: (l, r, 0, 0)) if is_sample else (lambda r: (l, CTX_MOD_ROW, 0, 0))
    return pl.pallas_call(
        functools.partial(_combine_kernel, n, cap, reqs),
        grid=(n_req // reqs,),
        in_specs=[pl.BlockSpec((reqs, slots, D_MODEL), lambda r: (r, 0, 0)),
                  pl.BlockSpec((reqs * n, LANES), lambda r: (r, 0)),
                  pl.BlockSpec((reqs * n, D_MODEL), lambda r: (r, 0)),
                  pl.BlockSpec((None, None, 1, 6 * D_MODEL), mod_row)],
        out_specs=pl.BlockSpec((reqs * n, D_MODEL), lambda r: (r, 0)),
        out_shape=jax.ShapeDtypeStruct(x1.shape, _F32),
        scratch_shapes=[pltpu.VMEM((reqs * n, slots), _BF16)],
        compiler_params=_params(1),
        name="combine_sample" if is_sample else "combine_prompt",
    )(o, rank_t, x1, mod4)


def _rope_tables(n):
    pos = np.arange(n)
    row = (pos // GRID_W).astype(np.float32)
    col = (pos % GRID_W).astype(np.float32)

    def axis_tab(half, p):
        freqs = (ROPE_BASE ** (-np.arange(half, dtype=np.float32) / half)).astype(np.float32)
        ang = p[:, None] * freqs[None, :]
        c, s = np.cos(ang), np.sin(ang)
        return np.concatenate([c, c], axis=1), np.concatenate([-s, s], axis=1)

    cr, sr = axis_tab(16, row)
    cc, sc = axis_tab(16, col)
    cos_swa = np.concatenate([cr, cc, cr, cc], axis=1)
    sin_swa = np.concatenate([sr, sc, sr, sc], axis=1)
    cr, sr = axis_tab(8, row)
    cc, sc = axis_tab(8, col)
    ones = np.ones((n, 64), np.float32)
    zeros = np.zeros((n, 64), np.float32)
    cos_mla = np.concatenate([ones, cr, cc, ones[:, :32]], axis=1)
    sin_mla = np.concatenate([zeros, sr, sc, zeros[:, :32]], axis=1)
    return tuple(jnp.asarray(t, dtype=_F32) for t in (cos_swa, sin_swa, cos_mla, sin_mla))


def kernel(x_prompt, x_sample, cache_swa_k, cache_swa_v, cache_mla_ckv, cache_mla_krope, c, c_ctx, w_mod, b_mod, norm1_g, norm2_g, w_in, pool_w, pool_scale, swa_q_g, swa_k_g, sink, mla_q_norm_g, w_q_b, mla_kv_norm_g, w_kv_b, mla_q_g, mla_k_g, w_br_a, w_br_b, w_br_c, w_out, w_router, w_gate, w_up, w_down):
    n_p, seq, _ = x_prompt.shape
    n_s, dec_seq, _ = x_sample.shape
    xp = x_prompt.reshape(n_p * seq, D_MODEL)
    xs = x_sample.reshape(n_s * dec_seq, D_MODEL)

    cond = jnp.concatenate([c, c_ctx[None, :], jnp.zeros((8 - n_s - 1, D_MODEL), _F32)], axis=0)
    mod4 = _modulation(cond, w_mod, b_mod).reshape(DEPTH, 8, 1, 6 * D_MODEL)

    tabs = _rope_tables(dec_seq)
    ck = cache_swa_k.reshape(n_s, DEPTH, PAST_LEN, 128)
    cv = cache_swa_v.reshape(n_s, DEPTH, PAST_LEN, 128)
    ckr = jnp.pad(cache_mla_krope, ((0, 0), (0, 0), (0, 0), (64, 32)))
    caches = (ck, cv, cache_mla_ckv, ckr)

    vec = lambda v: v.reshape(DEPTH, 1, -1)
    pad_head = lambda g: jnp.pad(g, ((0, 0), (0, LANES - MLA_QK_DIM)))
    wqb = jnp.pad(w_q_b.reshape(DEPTH, MLA_Q_RANK, MLA_HEADS, MLA_QK_DIM),
                  ((0, 0), (0, 0), (0, 0), (0, LANES - MLA_QK_DIM)))
    vecs1 = (vec(norm1_g), vec(jnp.tile(swa_q_g, (1, 2))), vec(jnp.tile(swa_k_g, (1, 2))),
             vec(mla_q_norm_g), vec(mla_kv_norm_g), vec(pad_head(mla_q_g)), vec(pad_head(mla_k_g)),
             vec(pool_scale))
    mats1 = (_bf(w_in[:, :, :ZA_WIDTH]), _bf(wqb.reshape(DEPTH, MLA_Q_RANK, MLA_HEADS * LANES)),
             _bf(w_kv_b), _bf(pool_w))
    vecs2 = (vec(norm1_g), vec(norm2_g))
    wg = _bf(w_in[:, :, GATE_OFF:])
    mats2 = (_bf(w_br_a), _bf(w_br_b), _bf(w_br_c), _bf(w_out),
             _bf(jnp.pad(w_router, ((0, 0), (0, 0), (0, LANES - N_EXPERTS)))))

    new_k, new_v, new_ckv, new_kr = [], [], [], []
    for l in range(DEPTH):
        out_p = _tm1(False, l, xp, mod4, vecs1, mats1, None)
        out_s = _tm1(True, l, xs, mod4, vecs1, mats1, (tabs, caches))
        new_k.append(out_p[7].reshape(n_p, seq, 2, 64))
        new_v.append(out_p[8].reshape(n_p, seq, 2, 64))
        new_ckv.append(out_p[9].reshape(n_p, seq, MLA_KV_RANK))
        new_kr.append(out_p[10].reshape(n_p, seq, MLA_ROPE_DIM))

        x1p, h2p, lgp = _tm2(False, l, xp, mod4, sink, vecs2, wg, out_p[:7], mats2)
        x1s, h2s, lgs = _tm2(True, l, xs, mod4, sink, vecs2, wg, out_s[:7], mats2)

        xg_p, vals_p, rt_p = _route(False, lgp, h2p)
        xg_s, vals_s, rt_s = _route(True, lgs, h2s)
        o_p, o_s = _moe(l, xg_p, vals_p, xg_s, vals_s, w_gate, w_up, w_down)
        xp = _combine(False, l, o_p, rt_p, x1p, mod4)
        xs = _combine(True, l, o_s, rt_s, x1s, mod4)

    return (xp.reshape(n_p, seq, D_MODEL), xs.reshape(n_s, dec_seq, D_MODEL),
            jnp.stack(new_k, axis=1), jnp.stack(new_v, axis=1),
            jnp.stack(new_ckv, axis=1), jnp.stack(new_kr, axis=1))
```

```python
import functools

import numpy as np
import jax
import jax.numpy as jnp
from jax import lax
from jax.experimental import pallas as pl
from jax.experimental.pallas import tpu as pltpu

D_MODEL = 1024
DEPTH = 2
GRID_W = 64
ROPE_BASE = 10000.0
EPS = 1e-6
NEG_INF = -1e30
POOL_WINDOWS = (2, 4, 8, 16)
POOL_WIDTH = 512
SWA_HEADS = 8
SWA_HEAD_DIM = 64
SWA_WINDOW = 128
MLA_HEADS = 8
MLA_Q_RANK = 384
MLA_KV_RANK = 256
MLA_NOPE_DIM = 64
MLA_ROPE_DIM = 32
MLA_QK_DIM = MLA_NOPE_DIM + MLA_ROPE_DIM
N_EXPERTS = 16
EXPERT_HIDDEN = 1024
CAPACITY_FACTOR = 2
PAST_LEN = 256

LANES = 128
ROW_BLOCK = 256
QUERY_BLOCK = 128
POOL_PAD = 16
ZA_WIDTH = 2048
GATE_OFF = 1952
VMEM_LIMIT = 56 * 1024 * 1024
CTX_MOD_ROW = 4
LOG2E = 1.4426950408889634

V_N1G, V_N2G, V_QG, V_KG, V_QNG, V_KVNG, V_MQG, V_MKG, V_PSC, VEC_WIDTH = (
    0, 1024, 2048, 2176, 2304, 2688, 2944, 3072, 3200, 3712)

_F32 = jnp.float32
_BF16 = jnp.bfloat16


def _bf(x):
    return x.astype(_BF16)


def _dot(a, b):
    return jnp.dot(a, b, preferred_element_type=_F32)


def _dot_nt(a, b):
    return lax.dot_general(a, b, (((1,), (1,)), ((), ())), preferred_element_type=_F32)


def _sigmoid(x):
    return 1.0 / (1.0 + jnp.exp(-x))


def _rms(x, g):
    return x * lax.rsqrt(jnp.mean(x * x, axis=-1, keepdims=True) + EPS) * g


def _mod_norm(x, gain_mod, shift):
    return _bf(x * lax.rsqrt(jnp.mean(x * x, axis=-1, keepdims=True) + EPS) * gain_mod + shift)


def _lane_iota(rows):
    return lax.broadcasted_iota(jnp.int32, (rows, LANES), 1)


def _swap_lanes(x, seg):
    lane = _lane_iota(x.shape[0])
    up = pltpu.roll(x, LANES - seg, 1)
    dn = pltpu.roll(x, seg, 1)
    return jnp.where((lane & seg) == 0, up, dn)


def _rope(x, cos, sin_signed, seg):
    return x * cos + _swap_lanes(x, seg) * sin_signed


def _seg_ones(seg, rows_used=LANES):
    r = lax.broadcasted_iota(jnp.int32, (LANES, LANES), 0)
    c = lax.broadcasted_iota(jnp.int32, (LANES, LANES), 1)
    return jnp.where(((r // seg) == (c // seg)) & (r < rows_used), 1.0, 0.0).astype(_BF16)


def _seg_sumsq(x, ones):
    return _dot(_bf(x * x), ones)


def _half_norm(x, gain, ones64=None):
    if ones64 is not None:
        return x * lax.rsqrt(_seg_sumsq(x, ones64) / SWA_HEAD_DIM + EPS) * gain
    lo = _lane_iota(x.shape[0]) < 64
    sq = x * x
    s_lo = jnp.sum(jnp.where(lo, sq, 0.0), axis=-1, keepdims=True)
    s_hi = jnp.sum(jnp.where(lo, 0.0, sq), axis=-1, keepdims=True)
    r = jnp.where(lo, lax.rsqrt(s_lo / SWA_HEAD_DIM + EPS), lax.rsqrt(s_hi / SWA_HEAD_DIM + EPS))
    return x * r * gain


def _dup_halves(x):
    lo = _lane_iota(x.shape[0]) < 64
    sw = pltpu.roll(x, 64, 1)
    return jnp.where(lo, x, sw), jnp.where(lo, sw, x)


def _const_spec(shape):
    return pl.BlockSpec(shape, lambda *_: (0,) * len(shape))


def _layer_spec(arr, l):
    return pl.BlockSpec((None,) + arr.shape[1:], lambda *_: (l,) + (0,) * (arr.ndim - 1))


def _params(n_axes):
    return pltpu.CompilerParams(dimension_semantics=("arbitrary",) * n_axes, vmem_limit_bytes=VMEM_LIMIT)


def _mod_kernel(cond_ref, w_ref, b_ref, o_ref):
    cnd = cond_ref[...]
    act = _bf(cnd * _sigmoid(cnd))
    o_ref[:, 0, :] = _dot(act, _bf(w_ref[...])) + b_ref[...]


def _modulation(cond, w_mod, b_mod):
    n_out = w_mod.shape[-1]
    tn = 1536
    return pl.pallas_call(
        _mod_kernel,
        grid=(DEPTH, n_out // tn),
        in_specs=[
            pl.BlockSpec((8, D_MODEL), lambda l, j: (0, 0)),
            pl.BlockSpec((None, D_MODEL, tn), lambda l, j: (l, 0, j)),
            pl.BlockSpec((None, 1, tn), lambda l, j: (l, 0, j)),
        ],
        out_specs=pl.BlockSpec((None, 8, 1, tn), lambda l, j: (l, 0, 0, j)),
        out_shape=jax.ShapeDtypeStruct((DEPTH, 8, 1, n_out), _F32),
        compiler_params=_params(2),
        name="modulation",
    )(cond, w_mod, b_mod.reshape(DEPTH, 1, n_out))


def _mla_kv_store(ckv, krr, wkvb_ref, mkg, rope_tabs, km_ref, vm_ref, rows, mxu_sums):
    n_rows = ckv.shape[0]
    lo = _lane_iota(n_rows) < 64
    kv = _dot(_bf(ckv), wkvb_ref[...])
    krg = krr * mkg
    if rope_tabs is not None:
        krg = _rope(krg, rope_tabs[0], rope_tabs[1], 8)
    if mxu_sums:
        ss_kr = _seg_sumsq(krr, _seg_ones(LANES))
        ones_lo = _seg_ones(LANES, 64)
        mkg_nope = jnp.where(_lane_iota(1) < 64, mkg, 0.0)
    else:
        ss_kr = jnp.sum(krr * krr, axis=-1, keepdims=True)
    for h in range(MLA_HEADS):
        slab = kv[:, LANES * h:LANES * (h + 1)]
        if mxu_sums:
            r = lax.rsqrt((_seg_sumsq(slab, ones_lo) + ss_kr) / MLA_QK_DIM + EPS)
            km_ref[rows, LANES * h:LANES * (h + 1)] = _bf((slab * mkg_nope + krg) * r)
        else:
            nope = jnp.where(lo, slab, 0.0)
            ss = jnp.sum(nope * nope, axis=-1, keepdims=True) + ss_kr
            r = lax.rsqrt(ss / MLA_QK_DIM + EPS)
            km_ref[rows, LANES * h:LANES * (h + 1)] = _bf((nope * mkg + krg) * r)
    for j in range(MLA_HEADS // 2):
        a = kv[:, 2 * LANES * j:2 * LANES * j + LANES]
        b = kv[:, 2 * LANES * j + LANES:2 * LANES * (j + 1)]
        vm_ref[rows, LANES * j:LANES * (j + 1)] = _bf(jnp.where(lo, pltpu.roll(a, 64, 1), b))


def _tm1_kernel(is_sample, n, reqs, *refs):
    if is_sample:
        (x_ref, mod_ref, vec_ref,
         wa_ref, wqb_ref, wkvb_ref, pw_ref,
         cs_ref, ss_ref, cm_ref, sm_ref, ck_ref, cv_ref, cckv_ref, ckr_ref,
         qs_ref, k_ref, v_ref, qm_ref, km_ref, vm_ref, ya_ref, pad_scr) = refs
    else:
        (x_ref, mod_ref, vec_ref,
         wa_ref, wqb_ref, wkvb_ref, pw_ref,
         qs_ref, k_ref, v_ref, qm_ref, km_ref, vm_ref, ya_ref,
         nk_ref, nv_ref, nckv_ref, nkr_ref, pad_scr) = refs

    sh1 = mod_ref[:, 0:D_MODEL]
    gm1 = vec_ref[:, V_N1G:V_N1G + D_MODEL] * (1.0 + mod_ref[:, D_MODEL:2 * D_MODEL])
    qg = vec_ref[:, V_QG:V_QG + LANES] * (SWA_HEAD_DIM ** -0.5 * LOG2E)
    kg = vec_ref[:, V_KG:V_KG + LANES]
    mqg = vec_ref[:, V_MQG:V_MQG + LANES] * (MLA_QK_DIM ** -0.5 * LOG2E)
    mkg = vec_ref[:, V_MKG:V_MKG + LANES]

    mxu_sums = is_sample
    ones64 = _seg_ones(64) if mxu_sums else None
    ones128 = _seg_ones(LANES) if mxu_sums else None
    bpr = n // ROW_BLOCK
    pad_rows = n + 2 * POOL_PAD
    zeros_pad = jnp.zeros((POOL_PAD, POOL_WIDTH), _F32)
    for q in range(reqs):
        pad_scr[pad_rows * q:pad_rows * q + POOL_PAD, :] = zeros_pad
        pad_scr[pad_rows * q + POOL_PAD + n:pad_rows * (q + 1), :] = zeros_pad

    def in_proj(j):
        x = x_ref[ROW_BLOCK * j:ROW_BLOCK * (j + 1), :]
        return _dot(_mod_norm(x, gm1, sh1), wa_ref[...])

    def sub_block(j, z):
        q = j // bpr
        tok = slice(ROW_BLOCK * j, ROW_BLOCK * (j + 1))
        rows = slice(ROW_BLOCK * (j % bpr), ROW_BLOCK * (j % bpr + 1))
        pad0 = pad_rows * q + POOL_PAD + rows.start
        pad_scr[pad0:pad0 + ROW_BLOCK, :] = z[:, 0:POOL_WIDTH]

        if is_sample:
            cs = cs_ref[rows, :]
            ss = ss_ref[rows, :]
            cm = cm_ref[rows, :]
            sm = sm_ref[rows, :]

        lo = _lane_iota(ROW_BLOCK) < 64
        for c in range(4):
            qn = _half_norm(z[:, 512 + LANES * c:512 + LANES * (c + 1)], qg, ones64)
            if is_sample:
                qn = _rope(qn, cs, ss, 16)
            qs_ref[q, 2 * c, rows, :] = _bf(jnp.where(lo, qn, 0.0))
            qs_ref[q, 2 * c + 1, rows, :] = _bf(jnp.where(lo, 0.0, qn))
        kn = _half_norm(z[:, 1024:1152], kg, ones64)
        vv = z[:, 1152:1280]
        if is_sample:
            kn = _rope(kn, cs, ss, 16)
        else:
            nk_ref[q] = kn.T
            nv_ref[q] = vv.T
        k0, k1 = _dup_halves(kn)
        k_ref[q, rows, 0:LANES] = _bf(k0)
        k_ref[q, rows, LANES:2 * LANES] = _bf(k1)
        v0, v1 = _dup_halves(vv)
        v_ref[q, rows, 0:LANES] = _bf(v0)
        v_ref[q, rows, LANES:2 * LANES] = _bf(v1)

        qcn = _bf(_rms(z[:, 1280:1664], vec_ref[:, V_QNG:V_QNG + MLA_Q_RANK]))
        qm = _dot(qcn, wqb_ref[...])
        for hh in range(MLA_HEADS):
            t = qm[:, LANES * hh:LANES * (hh + 1)]
            ss = _seg_sumsq(t, ones128) if mxu_sums else jnp.sum(t * t, axis=-1, keepdims=True)
            tn = t * lax.rsqrt(ss / MLA_QK_DIM + EPS) * mqg
            if is_sample:
                tn = _rope(tn, cm, sm, 8)
            qm_ref[tok, LANES * hh:LANES * (hh + 1)] = _bf(tn)

        ckv = _rms(z[:, 1664:1920], vec_ref[:, V_KVNG:V_KVNG + MLA_KV_RANK])
        krs = jnp.where(_lane_iota(ROW_BLOCK) < MLA_ROPE_DIM, z[:, 1920:2048], 0.0)
        if not is_sample:
            nckv_ref[tok, :] = ckv
            nkr_ref[q] = krs.T[0:MLA_ROPE_DIM, :]
        krr = pltpu.roll(krs, 64, 1)
        _mla_kv_store(ckv, krr, wkvb_ref, mkg, (cm, sm) if is_sample else None,
                      km_ref.at[q], vm_ref.at[q], rows, mxu_sums)

    n_blocks = reqs * bpr
    z_next = in_proj(0)
    for j in range(n_blocks):
        z_cur = z_next
        if j + 1 < n_blocks:
            z_next = in_proj(j + 1)
        sub_block(j, z_cur)

    if is_sample:
        ctx_rows = slice(n, n + PAST_LEN)
        c0, c1 = _dup_halves(ck_ref[...])
        k_ref[0, ctx_rows, 0:LANES] = _bf(c0)
        k_ref[0, ctx_rows, LANES:2 * LANES] = _bf(c1)
        c0, c1 = _dup_halves(cv_ref[...])
        v_ref[0, ctx_rows, 0:LANES] = _bf(c0)
        v_ref[0, ctx_rows, LANES:2 * LANES] = _bf(c1)
        _mla_kv_store(cckv_ref[...], ckr_ref[...], wkvb_ref, mkg, None, km_ref.at[0], vm_ref.at[0], ctx_rows,
                      mxu_sums)

    for j in range(n_blocks):
        q, c = j // bpr, j % bpr
        base = pad_rows * q + POOL_PAD + ROW_BLOCK * c
        t = lax.broadcasted_iota(jnp.int32, (ROW_BLOCK, 1), 0) + ROW_BLOCK * c
        for g, w in enumerate(POOL_WINDOWS):
            half = w // 2
            cols = slice(LANES * g, LANES * (g + 1))
            frame = pad_scr[base - 8:base + ROW_BLOCK + 8, cols]
            rows_f = ROW_BLOCK + 16
            run, span = frame, 1
            while span < w:
                run = run + pltpu.roll(run, span, 0)
                span *= 2
            if half > 1:
                run = pltpu.roll(run, rows_f - (half - 1), 0)
            acc = run[8:8 + ROW_BLOCK]
            u = frame[8:8 + ROW_BLOCK]
            cnt = (jnp.minimum(t + half, n) - jnp.maximum(t - half, 0)).astype(_F32)
            dlt = _bf(acc / cnt - u)
            ya = _dot(dlt, pw_ref[g]) * vec_ref[:, V_PSC + LANES * g:V_PSC + LANES * (g + 1)]
            ya_ref[ROW_BLOCK * j:ROW_BLOCK * (j + 1), cols] = _bf(ya)


def _tm1(is_sample, l, x2, mod4, vec, mats, extra):
    n = 1024 if is_sample else 256
    reqs = 1 if is_sample else 4
    n_req = x2.shape[0] // n
    nk = n + PAST_LEN if is_sample else n
    mod_row = (lambda r: (l, r, 0, 0)) if is_sample else (lambda r: (l, CTX_MOD_ROW, 0, 0))
    in_specs = [pl.BlockSpec((reqs * n, D_MODEL), lambda r: (r, 0)),
                pl.BlockSpec((None, None, 1, 6 * D_MODEL), mod_row),
                _layer_spec(vec, l)]
    in_specs += [_layer_spec(m, l) for m in mats]
    if is_sample:
        tabs, caches = extra
        in_specs += [_const_spec(t.shape) for t in tabs]
        in_specs += [pl.BlockSpec((None, None, PAST_LEN, c.shape[-1]), lambda r: (r, l, 0, 0)) for c in caches]
        operands = (x2, mod4, vec, *mats, *tabs, *caches)
    else:
        operands = (x2, mod4, vec, *mats)
    tok = n_req * n
    out_shape = [jax.ShapeDtypeStruct((n_req, SWA_HEADS, n, LANES), _BF16)]
    out_specs = [pl.BlockSpec((reqs, SWA_HEADS, n, LANES), lambda r: (r, 0, 0, 0))]
    for width in (256, 256):
        out_shape.append(jax.ShapeDtypeStruct((n_req, nk, width), _BF16))
        out_specs.append(pl.BlockSpec((reqs, nk, width), lambda r: (r, 0, 0)))
    out_shape.append(jax.ShapeDtypeStruct((tok, 1024), _BF16))
    out_specs.append(pl.BlockSpec((reqs * n, 1024), lambda r: (r, 0)))
    for width in (1024, 512):
        out_shape.append(jax.ShapeDtypeStruct((n_req, nk, width), _BF16))
        out_specs.append(pl.BlockSpec((reqs, nk, width), lambda r: (r, 0, 0)))
    out_shape.append(jax.ShapeDtypeStruct((tok, 512), _BF16))
    out_specs.append(pl.BlockSpec((reqs * n, 512), lambda r: (r, 0)))
    if not is_sample:
        for rows_t in (LANES, LANES):
            out_shape.append(jax.ShapeDtypeStruct((n_req, rows_t, n), _F32))
            out_specs.append(pl.BlockSpec((reqs, rows_t, n), lambda r: (r, 0, 0)))
        out_shape.append(jax.ShapeDtypeStruct((tok, MLA_KV_RANK), _F32))
        out_specs.append(pl.BlockSpec((reqs * n, MLA_KV_RANK), lambda r: (r, 0)))
        out_shape.append(jax.ShapeDtypeStruct((n_req, MLA_ROPE_DIM, n), _F32))
        out_specs.append(pl.BlockSpec((reqs, MLA_ROPE_DIM, n), lambda r: (r, 0, 0)))
    return pl.pallas_call(
        functools.partial(_tm1_kernel, is_sample, n, reqs),
        grid=(n_req // reqs,),
        in_specs=in_specs,
        out_specs=out_specs,
        out_shape=out_shape,
        scratch_shapes=[pltpu.VMEM((reqs * (n + 2 * POOL_PAD), POOL_WIDTH), _F32)],
        compiler_params=_params(1),
        name="tm1_sample" if is_sample else "tm1_prompt",
    )(*operands)


def _softmax_pv(parts, sink):
    m = None
    for s, _ in parts:
        mi = jnp.max(s, axis=-1, keepdims=True)
        m = mi if m is None else jnp.maximum(m, mi)
    if sink is not None:
        m = jnp.maximum(m, sink)
    den = None
    acc = None
    for s, v in parts:
        p = jnp.exp2(s - m)
        d = jnp.sum(p, axis=-1, keepdims=True)
        den = d if den is None else den + d
        o = _dot(_bf(p), v)
        acc = o if acc is None else acc + o
    if sink is not None:
        den = den + jnp.exp2(sink - m)
    return acc / den


def _tm2_kernel(is_sample, l, nsub, *refs):
    (sink_ref, x_ref, mod_ref, vec_ref, wg_ref, qs_ref, k_ref, v_ref, qm_ref, km_ref, vm_ref,
     ya_ref, wba_ref, wbb_ref, wbc_ref, wo_ref, wr_ref,
     x1_ref, h2_ref, lg_ref, yb_scr, yc_scr) = refs
    n_own = 1024
    q_rows = QUERY_BLOCK if is_sample else ROW_BLOCK
    grp_heads = SWA_HEADS // 2
    lo_q = _lane_iota(q_rows) < 64
    lo = _lane_iota(ROW_BLOCK) < 64
    head_of_row = lax.broadcasted_iota(jnp.int32, (grp_heads * q_rows, 1), 0) // q_rows
    sh1 = mod_ref[:, 0:D_MODEL]
    gm1 = vec_ref[:, V_N1G:V_N1G + D_MODEL] * (1.0 + mod_ref[:, D_MODEL:2 * D_MODEL])
    g1 = mod_ref[:, 2 * D_MODEL:3 * D_MODEL]
    sh2 = mod_ref[:, 3 * D_MODEL:4 * D_MODEL]
    gm2 = vec_ref[:, V_N2G:V_N2G + D_MODEL] * (1.0 + mod_ref[:, 4 * D_MODEL:5 * D_MODEL])

    for u in range(nsub):
        req = 0 if is_sample else u
        q_off = ROW_BLOCK * u if is_sample else 0
        tok = slice(ROW_BLOCK * u, ROW_BLOCK * (u + 1))
        k_r, v_r, km_r, vm_r = k_ref.at[req], v_ref.at[req], km_ref.at[req], vm_ref.at[req]

        def swa_chain_inputs(sb, kvh):
            rows = slice(q_off + q_rows * sb, q_off + q_rows * (sb + 1))
            kcols = slice(LANES * kvh, LANES * (kvh + 1))
            q4 = qs_ref[req, grp_heads * kvh:grp_heads * (kvh + 1), rows, :].reshape(grp_heads * q_rows, LANES)
            sink = jnp.full((grp_heads * q_rows, 1), sink_ref[l, grp_heads * kvh] * LOG2E, _F32)
            for hh in range(1, grp_heads):
                sink = jnp.where(head_of_row >= hh, sink_ref[l, grp_heads * kvh + hh] * LOG2E, sink)
            if not is_sample:
                return [(_dot_nt(q4, k_r[:, kcols]), v_r[:, kcols])], sink
            row_blk = (pl.program_id(0) * nsub + u) % (n_own // ROW_BLOCK)
            blk = row_blk * 2 + sb
            start = jnp.clip((blk - 1) * QUERY_BLOCK, 0, n_own - 3 * QUERY_BLOCK)
            start = pl.multiple_of(start, QUERY_BLOCK)
            shape = (grp_heads * QUERY_BLOCK, 3 * QUERY_BLOCK)
            qpos = blk * QUERY_BLOCK + (lax.broadcasted_iota(jnp.int32, shape, 0) & (QUERY_BLOCK - 1))
            kpos = start + lax.broadcasted_iota(jnp.int32, shape, 1)
            band = jnp.abs(kpos - qpos) <= SWA_WINDOW
            s_loc = _dot_nt(q4, k_r[pl.ds(start, 3 * QUERY_BLOCK), kcols])
            s_loc = jnp.where(band, s_loc, NEG_INF)
            s_ctx = _dot_nt(q4, k_r[n_own:n_own + PAST_LEN, kcols])
            return [(s_loc, v_r[pl.ds(start, 3 * QUERY_BLOCK), kcols]),
                    (s_ctx, v_r[n_own:n_own + PAST_LEN, kcols])], sink

        chains = [(sb, kvh) for sb in range(ROW_BLOCK // q_rows) for kvh in range(2)]
        nxt = swa_chain_inputs(*chains[0])
        for ci, (sb, kvh) in enumerate(chains):
            parts, sink = nxt
            if ci + 1 < len(chains):
                nxt = swa_chain_inputs(*chains[ci + 1])
            rows = slice(ROW_BLOCK * u + q_rows * sb, ROW_BLOCK * u + q_rows * (sb + 1))
            o4 = _softmax_pv(parts, sink)
            for pair in range(grp_heads // 2):
                even = o4[q_rows * 2 * pair:q_rows * (2 * pair + 1)]
                odd = o4[q_rows * (2 * pair + 1):q_rows * (2 * pair + 2)]
                col = LANES * (2 * kvh + pair)
                yb_scr[rows, col:col + LANES] = _bf(jnp.where(lo_q, even, odd))

        def mla_scores(hh):
            return _dot_nt(qm_ref[tok, LANES * hh:LANES * (hh + 1)], km_r[:, LANES * hh:LANES * (hh + 1)])

        lookahead = is_sample
        s_next = mla_scores(0)
        outs = []
        for hh in range(MLA_HEADS):
            s_cur = s_next if lookahead or hh == 0 else mla_scores(hh)
            if lookahead and hh + 1 < MLA_HEADS:
                s_next = mla_scores(hh + 1)
            j = hh // 2
            outs.append(_softmax_pv([(s_cur, vm_r[:, LANES * j:LANES * (j + 1)])], None))
            if hh % 2 == 1:
                yc_scr[tok, LANES * j:LANES * (j + 1)] = _bf(jnp.where(lo, outs[0], outs[1]))
                outs = []

        x = x_ref[tok, :]
        h = _mod_norm(x, gm1, sh1)
        merged = _sigmoid(_dot(h, wg_ref[:, 0:D_MODEL])) * _dot(ya_ref[tok, :], wba_ref[...])
        merged = merged + _sigmoid(_dot(h, wg_ref[:, D_MODEL:2 * D_MODEL])) * _dot(yb_scr[tok, :], wbb_ref[...])
        merged = merged + _sigmoid(_dot(h, wg_ref[:, 2 * D_MODEL:3 * D_MODEL])) * _dot(yc_scr[tok, :], wbc_ref[...])
        x1 = x + g1 * _dot(_bf(merged), wo_ref[...])
        x1_ref[tok, :] = x1
        h2 = _mod_norm(x1, gm2, sh2)
        h2_ref[tok, :] = h2
        lg_ref[tok, :] = _dot(h2, wr_ref[...])


def _tm2(is_sample, l, x2, mod4, sink, vec, wg, tm1_out, mats):
    qs, kd, vd, qm, km, vm, ya = tm1_out
    n = 1024 if is_sample else 256
    nsub = 2
    rows = nsub * ROW_BLOCK
    nk = kd.shape[1]
    tok = x2.shape[0]
    if is_sample:
        steps_per_req = n // rows
        mod_row = lambda i: (l, i // steps_per_req, 0, 0)
        qs_blk = pl.BlockSpec((1, SWA_HEADS, rows, LANES), lambda i: (i // steps_per_req, 0, i % steps_per_req, 0))
        req_blk = lambda width: pl.BlockSpec((1, nk, width), lambda i: (i // steps_per_req, 0, 0))
    else:
        mod_row = lambda i: (l, CTX_MOD_ROW, 0, 0)
        qs_blk = pl.BlockSpec((nsub, SWA_HEADS, ROW_BLOCK, LANES), lambda i: (i, 0, 0, 0))
        req_blk = lambda width: pl.BlockSpec((nsub, nk, width), lambda i: (i, 0, 0))
    row_blk = lambda width: pl.BlockSpec((rows, width), lambda i: (i, 0))
    in_specs = [pl.BlockSpec(memory_space=pltpu.SMEM),
                row_blk(D_MODEL),
                pl.BlockSpec((None, None, 1, 6 * D_MODEL), mod_row),
                _layer_spec(vec, l)]
    in_specs += [_layer_spec(wg, l), qs_blk, req_blk(256), req_blk(256),
                 row_blk(1024), req_blk(1024), req_blk(512), row_blk(512)]
    in_specs += [_layer_spec(m, l) for m in mats]
    return pl.pallas_call(
        functools.partial(_tm2_kernel, is_sample, l, nsub),
        grid=(tok // rows,),
        in_specs=in_specs,
        out_specs=[row_blk(D_MODEL), row_blk(D_MODEL), row_blk(LANES)],
        out_shape=[jax.ShapeDtypeStruct((tok, D_MODEL), _F32),
                   jax.ShapeDtypeStruct((tok, D_MODEL), _BF16),
                   jax.ShapeDtypeStruct((tok, LANES), _F32)],
        scratch_shapes=[pltpu.VMEM((rows, 512), _BF16), pltpu.VMEM((rows, 512), _BF16)],
        compiler_params=_params(1),
        name="tm2_sample" if is_sample else "tm2_prompt",
    )(sink, x2, mod4, vec, wg, qs, kd, vd, qm, km, vm, ya, *mats)


def _sort_lanes_desc(slabs):
    rows = slabs[0].shape[0]
    n = LANES * len(slabs)
    lane = _lane_iota(rows)
    k = 2
    while k <= n:
        j = k // 2
        while j >= 1:
            out = []
            for c, x in enumerate(slabs):
                if j < LANES:
                    partner = _swap_lanes(x, j)
                    if k < LANES:
                        take_max = (((lane & j) * (k // j)) ^ (lane & k)) == 0
                    elif (LANES * c) & k == 0:
                        take_max = (lane & j) == 0
                    else:
                        take_max = (lane & j) != 0
                    out.append(jnp.where(take_max, jnp.maximum(x, partner), jnp.minimum(x, partner)))
                else:
                    partner = slabs[c ^ (j // LANES)]
                    lower = (LANES * c) & j == 0
                    desc = (LANES * c) & k == 0
                    out.append(jnp.maximum(x, partner) if lower == desc else jnp.minimum(x, partner))
            slabs = out
            j //= 2
        k *= 2
    return slabs


def _route_kernel(n, cap, grp, lg_ref, h2_ref, tri_ref, xg_ref, vals_ref, rankt_ref,
                  a_scr, code_scr, pad_scr, p_scr):
    rg = N_EXPERTS * grp
    slots = N_EXPERTS * cap
    lane = _lane_iota(grp * n)
    logits = jnp.where(lane < N_EXPERTS, lg_ref[...], -jnp.inf)
    m = jnp.max(logits, axis=-1, keepdims=True)
    e = jnp.exp(logits - m)
    aff = e / jnp.sum(e, axis=-1, keepdims=True)
    for g in range(grp):
        a_scr[N_EXPERTS * g:N_EXPERTS * (g + 1), :] = aff[n * g:n * (g + 1), :].T[0:N_EXPERTS, :]
    a = a_scr[...]

    thr = _sort_lanes_desc([a[:, LANES * c:LANES * (c + 1)] for c in range(n // LANES)])[0][:, cap - 1:cap]
    capf = float(cap)

    gt = a > thr
    eqf = jnp.where(a == thr, 1.0, 0.0)
    need = capf - jnp.sum(jnp.where(gt, 1.0, 0.0), axis=1, keepdims=True)
    eq_before = _dot(_bf(eqf), tri_ref[...])
    sel = jnp.where(gt, 1.0, jnp.where(eq_before < need, eqf, 0.0))
    pos = _dot(_bf(sel), tri_ref[...])
    code_scr[...] = jnp.where(sel > 0.5, pos, float(n))

    pad_scr[N_EXPERTS:LANES, :] = jnp.full((LANES - N_EXPERTS, n), float(n), _F32)
    slot = lax.broadcasted_iota(jnp.int32, (cap, n), 0).astype(_F32)
    chunk = min(slots, 512)
    for g in range(grp):
        for ex in range(N_EXPERTS):
            r = N_EXPERTS * g + ex
            hit = code_scr[r:r + 1, :] == slot
            p_scr[slots * g + cap * ex:slots * g + cap * (ex + 1), :] = jnp.where(hit, 1.0, 0.0).astype(_BF16)
            vals_ref[g, cap * ex:cap * (ex + 1), :] = jnp.sum(
                jnp.where(hit, a_scr[r:r + 1, :], 0.0), axis=1, keepdims=True)
        for c in range(slots // chunk):
            rows = slice(chunk * c, chunk * (c + 1))
            xg_ref[g, rows, :] = _bf(_dot(p_scr[slots * g + chunk * c:slots * g + chunk * (c + 1), :],
                                          h2_ref[n * g:n * (g + 1), :]))
        pad_scr[0:N_EXPERTS, :] = code_scr[N_EXPERTS * g:N_EXPERTS * (g + 1), :]
        rankt_ref[n * g:n * (g + 1), :] = pad_scr[...].T


def _route(is_sample, logits, h2):
    n = 1024 if is_sample else 256
    grp = 2 if is_sample else 8
    cap = CAPACITY_FACTOR * n // N_EXPERTS
    n_req = logits.shape[0] // n
    slots = N_EXPERTS * cap
    tri = jnp.asarray(np.triu(np.ones((n, n), np.float32), k=1), dtype=_BF16)
    return pl.pallas_call(
        functools.partial(_route_kernel, n, cap, grp),
        grid=(n_req // grp,),
        in_specs=[pl.BlockSpec((grp * n, LANES), lambda r: (r, 0)),
                  pl.BlockSpec((grp * n, D_MODEL), lambda r: (r, 0)),
                  _const_spec((n, n))],
        out_specs=[pl.BlockSpec((grp, slots, D_MODEL), lambda r: (r, 0, 0)),
                   pl.BlockSpec((grp, slots, 1), lambda r: (r, 0, 0)),
                   pl.BlockSpec((grp * n, LANES), lambda r: (r, 0))],
        out_shape=[jax.ShapeDtypeStruct((n_req, slots, D_MODEL), _BF16),
                   jax.ShapeDtypeStruct((n_req, slots, 1), _F32),
                   jax.ShapeDtypeStruct((n_req * n, LANES), _F32)],
        scratch_shapes=[pltpu.VMEM((N_EXPERTS * grp, n), _F32), pltpu.VMEM((N_EXPERTS * grp, n), _F32),
                        pltpu.VMEM((LANES, n), _F32), pltpu.VMEM((grp * slots, n), _BF16)],
        compiler_params=_params(1),
        name="route_sample" if is_sample else "route_prompt",
    )(logits, h2, tri)


def _moe_kernel(xp_ref, vp_ref, xs_ref, vs_ref, wg_ref, wu_ref, wd_ref, op_ref, os_ref):
    wg = _bf(wg_ref[...])
    wu = _bf(wu_ref[...])
    wd = _bf(wd_ref[...])
    for x_ref, v_ref, o_ref in ((xp_ref, vp_ref, op_ref), (xs_ref, vs_ref, os_ref)):
        n_req, cap, _ = x_ref.shape
        xg = x_ref[...].reshape(n_req * cap, D_MODEL)
        a = _dot(xg, wg)
        u = _dot(xg, wu)
        act = _bf(a * _sigmoid(a) * u)
        scaled = _dot(act, wd) * v_ref[...].reshape(n_req * cap, 1)
        o_ref[...] = _bf(scaled).reshape(n_req, cap, D_MODEL)


def _moe(l, xg_p, vals_p, xg_s, vals_s, w_gate, w_up, w_down):
    def tok_specs(xg):
        n_req, slots, _ = xg.shape
        cap = slots // N_EXPERTS
        return (pl.BlockSpec((n_req, cap, D_MODEL), lambda e: (0, e, 0)),
                pl.BlockSpec((n_req, cap, 1), lambda e: (0, e, 0)))
    xp_spec, vp_spec = tok_specs(xg_p)
    xs_spec, vs_spec = tok_specs(xg_s)
    w_spec = pl.BlockSpec((None, None, D_MODEL, EXPERT_HIDDEN), lambda e: (l, e, 0, 0))
    return pl.pallas_call(
        _moe_kernel,
        grid=(N_EXPERTS,),
        in_specs=[xp_spec, vp_spec, xs_spec, vs_spec, w_spec, w_spec, w_spec],
        out_specs=[xp_spec, xs_spec],
        out_shape=[jax.ShapeDtypeStruct(xg_p.shape, _BF16), jax.ShapeDtypeStruct(xg_s.shape, _BF16)],
        compiler_params=_params(1),
        name="moe_experts",
    )(xg_p, vals_p, xg_s, vals_s, w_gate, w_up, w_down)


def _combine_kernel(n, cap, reqs, o_ref, rankt_ref, x1_ref, mod_ref, x2_ref, pt_scr):
    per = LANES // cap
    lane = _lane_iota(n)
    slot = (lane % cap).astype(_F32)
    g2 = mod_ref[:, 5 * D_MODEL:6 * D_MODEL]
    for r in range(reqs):
        rank_t = rankt_ref[n * r:n * (r + 1), :]
        for grp in range(N_EXPERTS // per):
            col = jnp.broadcast_to(rank_t[:, per * grp:per * grp + 1], (n, LANES))
            for k in range(1, per):
                nxt = jnp.broadcast_to(rank_t[:, per * grp + k:per * grp + k + 1], (n, LANES))
                col = jnp.where(lane >= cap * k, nxt, col)
            pt_scr[n * r:n * (r + 1), LANES * grp:LANES * (grp + 1)] = jnp.where(col == slot, 1.0, 0.0).astype(_BF16)
        for c in range(n // ROW_BLOCK):
            rows = slice(n * r + ROW_BLOCK * c, n * r + ROW_BLOCK * (c + 1))
            y = _dot(pt_scr[rows, :], o_ref[r])
            x2_ref[rows, :] = x1_ref[rows, :] + g2 * y


def _combine(is_sample, l, o, rank_t, x1, mod4):
    n = 1024 if is_sample else 256
    cap = CAPACITY_FACTOR * n // N_EXPERTS
    n_req, slots, _ = o.shape
    reqs = 1 if is_sample else 4
    mod_row = (lambda r: (l, r, 0, 0)) if is_sample else (lambda r: (l, CTX_MOD_ROW, 0, 0))
    return pl.pallas_call(
        functools.partial(_combine_kernel, n, cap, reqs),
        grid=(n_req // reqs,),
        in_specs=[pl.BlockSpec((reqs, slots, D_MODEL), lambda r: (r, 0, 0)),
                  pl.BlockSpec((reqs * n, LANES), lambda r: (r, 0)),
                  pl.BlockSpec((reqs * n, D_MODEL), lambda r: (r, 0)),
                  pl.BlockSpec((None, None, 1, 6 * D_MODEL), mod_row)],
        out_specs=pl.BlockSpec((reqs * n, D_MODEL), lambda r: (r, 0)),
        out_shape=jax.ShapeDtypeStruct(x1.shape, _F32),
        scratch_shapes=[pltpu.VMEM((reqs * n, slots), _BF16)],
        compiler_params=_params(1),
        name="combine_sample" if is_sample else "combine_prompt",
    )(o, rank_t, x1, mod4)


def _rope_tables(n):
    pos = np.arange(n)
    row = (pos // GRID_W).astype(np.float32)
    col = (pos % GRID_W).astype(np.float32)

    def axis_tab(half, p):
        freqs = (ROPE_BASE ** (-np.arange(half, dtype=np.float32) / half)).astype(np.float32)
        ang = p[:, None] * freqs[None, :]
        c, s = np.cos(ang), np.sin(ang)
        return np.concatenate([c, c], axis=1), np.concatenate([-s, s], axis=1)

    cr, sr = axis_tab(16, row)
    cc, sc = axis_tab(16, col)
    cos_swa = np.concatenate([cr, cc, cr, cc], axis=1)
    sin_swa = np.concatenate([sr, sc, sr, sc], axis=1)
    cr, sr = axis_tab(8, row)
    cc, sc = axis_tab(8, col)
    ones = np.ones((n, 64), np.float32)
    zeros = np.zeros((n, 64), np.float32)
    cos_mla = np.concatenate([ones, cr, cc, ones[:, :32]], axis=1)
    sin_mla = np.concatenate([zeros, sr, sc, zeros[:, :32]], axis=1)
    return tuple(jnp.asarray(t, dtype=_F32) for t in (cos_swa, sin_swa, cos_mla, sin_mla))


def kernel(x_prompt, x_sample, cache_swa_k, cache_swa_v, cache_mla_ckv, cache_mla_krope, c, c_ctx, w_mod, b_mod, norm1_g, norm2_g, w_in, pool_w, pool_scale, swa_q_g, swa_k_g, sink, mla_q_norm_g, w_q_b, mla_kv_norm_g, w_kv_b, mla_q_g, mla_k_g, w_br_a, w_br_b, w_br_c, w_out, w_router, w_gate, w_up, w_down):
    n_p, seq, _ = x_prompt.shape
    n_s, dec_seq, _ = x_sample.shape
    xp = x_prompt.reshape(n_p * seq, D_MODEL)
    xs = x_sample.reshape(n_s * dec_seq, D_MODEL)

    cond = jnp.concatenate([c, c_ctx[None, :], jnp.zeros((8 - n_s - 1, D_MODEL), _F32)], axis=0)
    mod4 = _modulation(cond, w_mod, b_mod)

    tabs = _rope_tables(dec_seq)
    ck = cache_swa_k.reshape(n_s, DEPTH, PAST_LEN, 128)
    cv = cache_swa_v.reshape(n_s, DEPTH, PAST_LEN, 128)
    ckr = jnp.pad(cache_mla_krope, ((0, 0), (0, 0), (0, 0), (64, 32)))
    caches = (ck, cv, cache_mla_ckv, ckr)

    pad_head = lambda g: jnp.pad(g, ((0, 0), (0, LANES - MLA_QK_DIM)))
    wqb = jnp.pad(w_q_b.reshape(DEPTH, MLA_Q_RANK, MLA_HEADS, MLA_QK_DIM),
                  ((0, 0), (0, 0), (0, 0), (0, LANES - MLA_QK_DIM)))
    vec = jnp.concatenate([norm1_g, norm2_g, swa_q_g, swa_q_g, swa_k_g, swa_k_g, mla_q_norm_g, mla_kv_norm_g,
                           pad_head(mla_q_g), pad_head(mla_k_g), pool_scale], axis=1).reshape(DEPTH, 1, VEC_WIDTH)
    mats1 = (_bf(w_in[:, :, :ZA_WIDTH]), _bf(wqb.reshape(DEPTH, MLA_Q_RANK, MLA_HEADS * LANES)),
             _bf(w_kv_b), _bf(pool_w))
    wg = _bf(w_in[:, :, GATE_OFF:])
    mats2 = (_bf(w_br_a), _bf(w_br_b), _bf(w_br_c), _bf(w_out),
             _bf(jnp.pad(w_router, ((0, 0), (0, 0), (0, LANES - N_EXPERTS)))))

    new_k, new_v, new_ckv, new_kr = [], [], [], []
    for l in range(DEPTH):
        out_p = _tm1(False, l, xp, mod4, vec, mats1, None)
        out_s = _tm1(True, l, xs, mod4, vec, mats1, (tabs, caches))
        new_k.append(out_p[7])
        new_v.append(out_p[8])
        new_ckv.append(out_p[9].reshape(n_p, seq, MLA_KV_RANK))
        new_kr.append(out_p[10])

        x1p, h2p, lgp = _tm2(False, l, xp, mod4, sink, vec, wg, out_p[:7], mats2)
        x1s, h2s, lgs = _tm2(True, l, xs, mod4, sink, vec, wg, out_s[:7], mats2)

        xg_p, vals_p, rt_p = _route(False, lgp, h2p)
        xg_s, vals_s, rt_s = _route(True, lgs, h2s)
        o_p, o_s = _moe(l, xg_p, vals_p, xg_s, vals_s, w_gate, w_up, w_down)
        xp = _combine(False, l, o_p, rt_p, x1p, mod4)
        xs = _combine(True, l, o_s, rt_s, x1s, mod4)

    def heads_last(stacked):
        return stacked.reshape(n_p, DEPTH, 2, SWA_HEAD_DIM, seq).transpose(0, 1, 4, 2, 3)

    return (xp.reshape(n_p, seq, D_MODEL), xs.reshape(n_s, dec_seq, D_MODEL),
            heads_last(jnp.stack(new_k, axis=1)), heads_last(jnp.stack(new_v, axis=1)),
            jnp.stack(new_ckv, axis=1), jnp.stack(new_kr, axis=1).transpose(0, 1, 3, 2))
```
